```python
import math
import jax, jax.numpy as jnp
from jax import lax
import numpy as np

D_MODEL = 1024
BATCH = 4
SEQ = 8192
DEPTH = 1

ROPE_THETA = 500000.0
EPS = 1e-6
Q_BLOCK = 128
D_FF = 2816

MLA_HEADS = 8
MLA_Q_RANK = 256
MLA_KV_RANK = 128
MLA_NOPE = 64
MLA_ROPE = 32
MLA_V = 64
MLA_WIDTH = MLA_HEADS * MLA_V

DIFF_HEADS = 4
DIFF_HEAD_DIM = 64
DIFF_ROT = DIFF_HEAD_DIM // 4
DIFF_WIDTH = DIFF_HEADS * 2 * DIFF_HEAD_DIM

IN_SPLITS = (MLA_Q_RANK, MLA_KV_RANK, MLA_ROPE, DIFF_WIDTH, DIFF_WIDTH, DIFF_WIDTH, D_MODEL, D_MODEL)
IN_COLS = sum(IN_SPLITS)

kernel_name = "hybrid_mla_diffattn_gated_macaron"


def rms_norm(x, w):
    xf = x.astype(jnp.float32)
    y = xf * lax.rsqrt(jnp.mean(xf * xf, axis=-1, keepdims=True) + EPS)
    return (y * w.astype(jnp.float32)).astype(x.dtype)


def swiglu(x, w_gate, w_up, w_down):
    return (jax.nn.silu(x @ w_gate) * (x @ w_up)) @ w_down


def apply_rope(x, positions, rot_dim):
    half = rot_dim // 2
    inv_freq = jnp.power(jnp.float32(ROPE_THETA), -2.0 * jnp.arange(half, dtype=jnp.float32) / rot_dim)
    ang = positions.astype(jnp.float32)[..., None] * inv_freq
    cos = jnp.cos(ang)[:, :, None, :]
    sin = jnp.sin(ang)[:, :, None, :]
    xr = x[..., :rot_dim].astype(jnp.float32)
    x1, x2 = xr[..., :half], xr[..., half:]
    rot = jnp.concatenate([x1 * cos - x2 * sin, x2 * cos + x1 * sin], axis=-1).astype(x.dtype)
    return jnp.concatenate([rot, x[..., rot_dim:]], axis=-1)


def to_blocks(t):
    b, s, h, d = t.shape
    return t.reshape(b, s // Q_BLOCK, Q_BLOCK, h, d).transpose(1, 0, 3, 2, 4)


def from_blocks(t):
    nb, b, h, qb, d = t.shape
    return t.transpose(1, 0, 3, 2, 4).reshape(b, nb * qb, h, d)


def causal_probs(q_blk, k, blk_idx, scale):
    s = jnp.einsum('bhqd,bhkd->bhqk', q_blk, k).astype(jnp.float32) * scale
    q_pos = blk_idx * Q_BLOCK + jnp.arange(Q_BLOCK)
    k_pos = jnp.arange(k.shape[2])
    mask = k_pos[None, :] <= q_pos[:, None]
    s = jnp.where(mask, s, -jnp.inf)
    return jax.nn.softmax(s, axis=-1)


def mla_branch(c_q, c_kv, k_rope_raw, positions, q_norm, w_uq, kv_norm, w_ukv):
    b, s, _ = c_q.shape
    q = (rms_norm(c_q, q_norm) @ w_uq).reshape(b, s, MLA_HEADS, MLA_NOPE + MLA_ROPE)
    q = jnp.concatenate([q[..., :MLA_NOPE], apply_rope(q[..., MLA_NOPE:], positions, MLA_ROPE)], axis=-1)
    kv = (rms_norm(c_kv, kv_norm) @ w_ukv).reshape(b, s, MLA_HEADS, MLA_NOPE + MLA_V)
    k_nope, v = kv[..., :MLA_NOPE], kv[..., MLA_NOPE:]
    k_rope = apply_rope(k_rope_raw.reshape(b, s, 1, MLA_ROPE), positions, MLA_ROPE)
    k = jnp.concatenate([k_nope, jnp.broadcast_to(k_rope, (b, s, MLA_HEADS, MLA_ROPE))], axis=-1)
    kt = k.transpose(0, 2, 1, 3)
    vt = v.transpose(0, 2, 1, 3)
    scale = (MLA_NOPE + MLA_ROPE) ** -0.5

    def block(args):
        qb, i = args
        p = causal_probs(qb, kt, i, scale)
        return jnp.einsum('bhqk,bhkd->bhqd', p.astype(vt.dtype), vt)

    out = lax.map(block, (to_blocks(q), jnp.arange(s // Q_BLOCK)))
    return from_blocks(out).reshape(b, s, MLA_WIDTH)


def diff_branch(q, k, v, positions, lam_q1, lam_k1, lam_q2, lam_k2, subln, layer_idx):
    b, s, _ = q.shape
    hd = DIFF_HEAD_DIM
    q = apply_rope(q.reshape(b, s, 2 * DIFF_HEADS, hd), positions, DIFF_ROT).reshape(b, s, DIFF_HEADS, 2, hd)
    k = apply_rope(k.reshape(b, s, 2 * DIFF_HEADS, hd), positions, DIFF_ROT).reshape(b, s, DIFF_HEADS, 2, hd)
    v = v.reshape(b, s, DIFF_HEADS, 2 * hd)
    q1, q2 = q[..., 0, :], q[..., 1, :]
    k1t = k[..., 0, :].transpose(0, 2, 1, 3)
    k2t = k[..., 1, :].transpose(0, 2, 1, 3)
    vt = v.transpose(0, 2, 1, 3)
    lam_init = 0.8 - 0.6 * math.exp(-0.3 * (layer_idx - 1))
    f32 = jnp.float32
    lam = (jnp.exp(jnp.sum(lam_q1.astype(f32) * lam_k1.astype(f32)))
           - jnp.exp(jnp.sum(lam_q2.astype(f32) * lam_k2.astype(f32))) + lam_init)
    scale = hd ** -0.5

    def block(args):
        q1b, q2b, i = args
        a = causal_probs(q1b, k1t, i, scale) - lam * causal_probs(q2b, k2t, i, scale)
        return jnp.einsum('bhqk,bhkd->bhqd', a.astype(vt.dtype), vt)

    out = from_blocks(lax.map(block, (to_blocks(q1), to_blocks(q2), jnp.arange(s // Q_BLOCK))))
    out = rms_norm(out, subln) * (1.0 - lam_init)
    return out.reshape(b, s, DIFF_WIDTH)


def hybrid_layer(x, positions, layer_idx,
                 ffn1_norm, ffn1_w_gate, ffn1_w_up, ffn1_w_down,
                 mix_norm, w_in,
                 mla_q_norm, mla_w_uq, mla_kv_norm, mla_w_ukv,
                 diff_lam_q1, diff_lam_k1, diff_lam_q2, diff_lam_k2, diff_subln,
                 w_proj_mla, w_proj_diff, w_out,
                 ffn2_norm, ffn2_w_gate, ffn2_w_up, ffn2_w_down):
    x = x + 0.5 * swiglu(rms_norm(x, ffn1_norm), ffn1_w_gate, ffn1_w_up, ffn1_w_down)

    h = rms_norm(x, mix_norm)
    z = h @ w_in
    cuts = []
    acc = 0
    for n in IN_SPLITS[:-1]:
        acc += n
        cuts.append(acc)
    c_q, c_kv, k_rope_raw, dq, dk, dv, gate_mla, gate_diff = jnp.split(z, cuts, axis=-1)

    y_mla = mla_branch(c_q, c_kv, k_rope_raw, positions, mla_q_norm, mla_w_uq, mla_kv_norm, mla_w_ukv)
    y_diff = diff_branch(dq, dk, dv, positions, diff_lam_q1, diff_lam_k1, diff_lam_q2, diff_lam_k2,
                         diff_subln, layer_idx)

    merged = (jax.nn.sigmoid(gate_mla) * (y_mla @ w_proj_mla)
              + jax.nn.sigmoid(gate_diff) * (y_diff @ w_proj_diff))
    x = x + merged @ w_out

    x = x + 0.5 * swiglu(rms_norm(x, ffn2_norm), ffn2_w_gate, ffn2_w_up, ffn2_w_down)
    return x


def setup_inputs(seed: int = 0) -> dict:
    key = jax.random.key(seed)
    ks = jax.random.split(key, 32)
    f32 = jnp.float32

    def w(k, fan_in, fan_out):
        return jax.random.normal(k, (DEPTH, fan_in, fan_out), f32) * fan_in ** -0.5

    def gain(k, n):
        return 1.0 + 0.01 * jax.random.normal(k, (DEPTH, n), f32)

    x = jax.random.normal(ks[0], (BATCH, SEQ, D_MODEL), f32)
    offsets = jax.random.randint(ks[1], (BATCH, 1), 0, 1024, dtype=jnp.int32)
    positions = jnp.arange(SEQ, dtype=jnp.int32)[None, :] + offsets
    return {
        "x": x,
        "positions": positions,
        "ffn1_norm": gain(ks[2], D_MODEL),
        "ffn1_w_gate": w(ks[3], D_MODEL, D_FF),
        "ffn1_w_up": w(ks[4], D_MODEL, D_FF),
        "ffn1_w_down": w(ks[5], D_FF, D_MODEL),
        "mix_norm": gain(ks[6], D_MODEL),
        "w_in": w(ks[7], D_MODEL, IN_COLS),
        "mla_q_norm": gain(ks[8], MLA_Q_RANK),
        "mla_w_uq": w(ks[9], MLA_Q_RANK, MLA_HEADS * (MLA_NOPE + MLA_ROPE)),
        "mla_kv_norm": gain(ks[10], MLA_KV_RANK),
        "mla_w_ukv": w(ks[11], MLA_KV_RANK, MLA_HEADS * (MLA_NOPE + MLA_V)),
        "diff_lam_q1": 0.1 * jax.random.normal(ks[12], (DEPTH, DIFF_HEAD_DIM), f32),
        "diff_lam_k1": 0.1 * jax.random.normal(ks[13], (DEPTH, DIFF_HEAD_DIM), f32),
        "diff_lam_q2": 0.1 * jax.random.normal(ks[14], (DEPTH, DIFF_HEAD_DIM), f32),
        "diff_lam_k2": 0.1 * jax.random.normal(ks[15], (DEPTH, DIFF_HEAD_DIM), f32),
        "diff_subln": gain(ks[16], 2 * DIFF_HEAD_DIM),
        "w_proj_mla": w(ks[17], MLA_WIDTH, D_MODEL),
        "w_proj_diff": w(ks[18], DIFF_WIDTH, D_MODEL),
        "w_out": w(ks[19], D_MODEL, D_MODEL),
        "ffn2_norm": gain(ks[20], D_MODEL),
        "ffn2_w_gate": w(ks[21], D_MODEL, D_FF),
        "ffn2_w_up": w(ks[22], D_MODEL, D_FF),
        "ffn2_w_down": w(ks[23], D_FF, D_MODEL),
        "final_norm": 1.0 + 0.01 * jax.random.normal(ks[24], (D_MODEL,), f32),
    }


def reference(x, positions,
              ffn1_norm, ffn1_w_gate, ffn1_w_up, ffn1_w_down,
              mix_norm, w_in,
              mla_q_norm, mla_w_uq, mla_kv_norm, mla_w_ukv,
              diff_lam_q1, diff_lam_k1, diff_lam_q2, diff_lam_k2, diff_subln,
              w_proj_mla, w_proj_diff, w_out,
              ffn2_norm, ffn2_w_gate, ffn2_w_up, ffn2_w_down,
              final_norm):
    for l in range(DEPTH):
        x = hybrid_layer(
            x, positions, l + 1,
            ffn1_norm[l], ffn1_w_gate[l], ffn1_w_up[l], ffn1_w_down[l],
            mix_norm[l], w_in[l],
            mla_q_norm[l], mla_w_uq[l], mla_kv_norm[l], mla_w_ukv[l],
            diff_lam_q1[l], diff_lam_k1[l], diff_lam_q2[l], diff_lam_k2[l], diff_subln[l],
            w_proj_mla[l], w_proj_diff[l], w_out[l],
            ffn2_norm[l], ffn2_w_gate[l], ffn2_w_up[l], ffn2_w_down[l])
    return rms_norm(x, final_norm)
```

```python
import functools
import math

import numpy as np
import jax
import jax.numpy as jnp
from jax import lax
from jax.experimental import pallas as pl
from jax.experimental.pallas import tpu as pltpu

F32 = jnp.float32
BF16 = jnp.bfloat16

D_MODEL = 1024
D_FF = 2816
ROPE_THETA = 500000.0
EPS = 1e-6

MLA_HEADS = 8
MLA_Q_RANK = 256
MLA_KV_RANK = 128
MLA_NOPE = 64
MLA_ROPE = 32
MLA_V = 64

DIFF_HEADS = 4
DIFF_HEAD_DIM = 64
DIFF_ROT = DIFF_HEAD_DIM // 4
DIFF_WIDTH = DIFF_HEADS * 2 * DIFF_HEAD_DIM

LANES = 128
FF_CHUNK = 256
N_FF_CHUNKS = D_FF // FF_CHUNK
VMEM_LIMIT = 56 * 1024 * 1024

C_CQ = 0
C_CKV = C_CQ + MLA_Q_RANK
C_KRA = C_CKV + MLA_KV_RANK
C_KRB = C_KRA + LANES
C_DQ = C_KRB + LANES
C_DQS = C_DQ + DIFF_WIDTH
C_DK = C_DQS + DIFF_WIDTH
C_DKS = C_DK + DIFF_WIDTH
C_DV = C_DKS + DIFF_WIDTH
C_GATE = C_DV + DIFF_WIDTH
N_IN_COLS = C_GATE + 2 * D_MODEL


def _rms(xf, w):
    ms = jnp.mean(xf * xf, axis=-1, keepdims=True)
    return xf * lax.rsqrt(ms + EPS) * w


def _dot(a, b):
    return jnp.dot(a, b, preferred_element_type=F32)


def _resident(shape):
    nd = len(shape)
    return pl.BlockSpec(shape, lambda *_: (0,) * nd, pipeline_mode=pl.Buffered(1))


def _ffn_kernel(x_ref, nw_ref, wg_ref, wu_ref, wd_ref, fw_ref, o_ref, xn_ref, acc_ref, *, final):
    x = x_ref[...]
    xn_ref[...] = _rms(x, nw_ref[...]).astype(BF16)
    acc_ref[...] = jnp.zeros_like(acc_ref)

    def body(c, carry):
        xn = xn_ref[...]
        g = _dot(xn, wg_ref[c])
        u = _dot(xn, wu_ref[c])
        a = (g * jax.nn.sigmoid(g) * u).astype(BF16)
        acc_ref[...] += _dot(a, wd_ref[c])
        return carry

    lax.fori_loop(0, N_FF_CHUNKS, body, 0)
    y = x + 0.5 * acc_ref[...]
    if final:
        y = _rms(y, fw_ref[...])
    o_ref[...] = y


def _ffn(x2d, norm_w, wg, wu, wd, final_w, *, final, tm):
    n = x2d.shape[0]
    row = pl.BlockSpec((tm, D_MODEL), lambda i: (i, 0))
    return pl.pallas_call(
        functools.partial(_ffn_kernel, final=final),
        grid=(n // tm,),
        in_specs=[row, _resident((1, D_MODEL)), _resident(wg.shape), _resident(wu.shape),
                  _resident(wd.shape), _resident((1, D_MODEL))],
        out_specs=row,
        out_shape=jax.ShapeDtypeStruct((n, D_MODEL), F32),
        scratch_shapes=[pltpu.VMEM((tm, D_MODEL), BF16), pltpu.VMEM((tm, D_MODEL), F32)],
        compiler_params=pltpu.CompilerParams(dimension_semantics=("arbitrary",),
                                             vmem_limit_bytes=VMEM_LIMIT),
        name="ffn_final" if final else "ffn",
    )(x2d, norm_w, wg, wu, wd, final_w)


def _inproj_kernel(x_ref, pos_ref, mixw_ref, win_ref, qnw_ref, wuqa_ref, wuqb_ref, kvnw_ref,
                   wuk_ref, wuv_ref, rot_ref,
                   qm_ref, km_ref, vm_ref, qd1_ref, qd2_ref, kd_ref, vd_ref, g_ref):
    h = _rms(x_ref[...], mixw_ref[...]).astype(BF16)
    pos = pos_ref[...]
    rot = rot_ref[...]
    ang_m = pos * rot[0:1]
    cos_m = jnp.cos(ang_m)
    sin_m = jnp.sin(ang_m) * rot[1:2]
    ang_d = pos * rot[2:3]
    cos_d = jnp.cos(ang_d)
    sin_d = jnp.sin(ang_d) * rot[3:4]
    lo, hi = rot[4:5], rot[5:6]

    z0 = _dot(h, win_ref[:, C_CQ:C_DQ])
    cqn = _rms(z0[:, C_CQ:C_CKV], qnw_ref[...]).astype(BF16)
    kvn = _rms(z0[:, C_CKV:C_KRA], kvnw_ref[...]).astype(BF16)
    k_rope = z0[:, C_KRA:C_KRB] * cos_m + z0[:, C_KRB:C_DQ] * sin_m

    qa = _dot(cqn, wuqa_ref[...])
    qb = _dot(cqn, wuqb_ref[...])
    ka = _dot(kvn, wuk_ref[...])
    for hd in range(MLA_HEADS):
        sl = slice(hd * LANES, (hd + 1) * LANES)
        qm_ref[:, sl] = (qa[:, sl] * cos_m + qb[:, sl] * sin_m).astype(BF16)
        km_ref[:, sl] = (ka[:, sl] + k_rope).astype(BF16)
    vm_ref[...] = _dot(kvn, wuv_ref[...]).astype(BF16)

    dq = _dot(h, win_ref[:, C_DQ:C_DQS])
    dqs = _dot(h, win_ref[:, C_DQS:C_DK])
    dk = _dot(h, win_ref[:, C_DK:C_DKS])
    dks = _dot(h, win_ref[:, C_DKS:C_DV])
    for hd in range(DIFF_HEADS):
        sl = slice(hd * LANES, (hd + 1) * LANES)
        qr = dq[:, sl] * cos_d + dqs[:, sl] * sin_d
        qd1_ref[:, sl] = (qr * lo).astype(BF16)
        qd2_ref[:, sl] = (qr * hi).astype(BF16)
        kd_ref[:, sl] = (dk[:, sl] * cos_d + dks[:, sl] * sin_d).astype(BF16)
    vd_ref[...] = _dot(h, win_ref[:, C_DV:C_GATE]).astype(BF16)
    g_ref[...] = jax.nn.sigmoid(_dot(h, win_ref[:, C_GATE:N_IN_COLS])).astype(BF16)


def _inproj(x1, posb, mixw, win, qnw, wuqa, wuqb, kvnw, wuk, wuv, rot, *, tm):
    n = x1.shape[0]

    def row(w):
        return pl.BlockSpec((tm, w), lambda i: (i, 0))

    wide = MLA_HEADS * LANES
    out_widths = (wide, wide, wide, DIFF_WIDTH, DIFF_WIDTH, DIFF_WIDTH, DIFF_WIDTH, 2 * D_MODEL)
    return pl.pallas_call(
        _inproj_kernel,
        grid=(n // tm,),
        in_specs=[row(D_MODEL), row(LANES), _resident(mixw.shape), _resident(win.shape),
                  _resident(qnw.shape), _resident(wuqa.shape), _resident(wuqb.shape),
                  _resident(kvnw.shape), _resident(wuk.shape), _resident(wuv.shape),
                  _resident(rot.shape)],
        out_specs=[row(w) for w in out_widths],
        out_shape=[jax.ShapeDtypeStruct((n, w), BF16) for w in out_widths],
        compiler_params=pltpu.CompilerParams(dimension_semantics=("arbitrary",),
                                             vmem_limit_bytes=VMEM_LIMIT),
        name="inproj",
    )(x1, posb, mixw, win, qnw, wuqa, wuqb, kvnw, wuk, wuv, rot)


def _flash_kernel(*refs, tq, tk, n_q, finish):
    q_refs = refs[:n_q]
    k_ref, v_ref = refs[n_q], refs[n_q + 1]
    extra = refs[n_q + 2:-4]
    o_ref, m_ref, l_ref, acc_ref = refs[-4:]
    qi = pl.program_id(2)
    rows = n_q * tq

    q = jnp.concatenate([r[0] for r in q_refs], axis=0) if n_q > 1 else q_refs[0][0]
    m_ref[...] = jnp.full_like(m_ref, -jnp.inf)
    l_ref[...] = jnp.zeros_like(l_ref)
    acc_ref[...] = jnp.zeros_like(acc_ref)

    def step(start, diag_offset):
        k = k_ref[0, pl.ds(start, tk), :]
        v = v_ref[0, pl.ds(start, tk), :]
        s = lax.dot_general(q, k, (((1,), (1,)), ((), ())), preferred_element_type=F32)
        if diag_offset is not None:
            r = lax.broadcasted_iota(jnp.int32, (rows, tk), 0)
            if n_q > 1:
                r = jnp.where(r >= tq, r - tq, r)
            c = lax.broadcasted_iota(jnp.int32, (rows, tk), 1) + diag_offset
            s = jnp.where(c <= r, s, -jnp.inf)
        m_prev = m_ref[...]
        m_new = jnp.maximum(m_prev, jnp.max(s, axis=1, keepdims=True))
        alpha = jnp.exp(m_prev - m_new)
        p = jnp.exp(s - m_new)
        l_ref[...] = alpha * l_ref[...] + jnp.sum(p, axis=1, keepdims=True)
        acc_ref[...] = alpha * acc_ref[...] + _dot(p.astype(BF16), v)
        m_ref[...] = m_new

    def full_step(j, carry):
        step(pl.multiple_of(j * tk, tk), None)
        return carry

    lax.fori_loop(0, qi * (tq // tk), full_step, 0)
    for d in range(tq // tk):
        step(pl.multiple_of(qi * tq + d * tk, tk), d * tk)

    o = acc_ref[...] / l_ref[...]
    o_ref[0] = finish([o[i * tq:(i + 1) * tq] for i in range(n_q)], *extra).astype(o_ref.dtype)


def _flash(qs, k, v, extra, *, heads, tq, tk, finish, name):
    b, s, _ = k.shape
    n_q = len(qs)
    q_spec = pl.BlockSpec((1, tq, LANES), lambda bi, hi, qi: (bi, qi, hi))
    kv_spec = pl.BlockSpec((1, s, LANES), lambda bi, hi, qi: (bi, 0, hi))
    return pl.pallas_call(
        functools.partial(_flash_kernel, tq=tq, tk=tk, n_q=n_q, finish=finish),
        grid=(b, heads, s // tq),
        in_specs=[q_spec] * n_q + [kv_spec, kv_spec] + [_resident(e.shape) for e in extra],
        out_specs=q_spec,
        out_shape=jax.ShapeDtypeStruct((b, s, heads * LANES), BF16),
        scratch_shapes=[pltpu.VMEM((n_q * tq, 1), F32), pltpu.VMEM((n_q * tq, 1), F32),
                        pltpu.VMEM((n_q * tq, LANES), F32)],
        compiler_params=pltpu.CompilerParams(
            dimension_semantics=("arbitrary", "arbitrary", "arbitrary"),
            vmem_limit_bytes=VMEM_LIMIT),
        name=name,
    )(*qs, k, v, *extra)


def _mla_finish(outs):
    return outs[0]


def _diff_finish(outs, lam_ref, subln_ref, *, lam_init):
    lam_v = lam_ref[...]
    e1 = jnp.exp(jnp.sum(lam_v[0:1] * lam_v[1:2], axis=1, keepdims=True))
    e2 = jnp.exp(jnp.sum(lam_v[2:3] * lam_v[3:4], axis=1, keepdims=True))
    lam = e1 - e2 + lam_init
    o = outs[0] - lam * outs[1]
    return _rms(o, subln_ref[...]) * (1.0 - lam_init)


def _merge_kernel(x_ref, ym_ref, yd_ref, g_ref, pm_ref, pd_ref, wo_ref, o_ref):
    g = g_ref[...]
    m = (g[:, :D_MODEL] * _dot(ym_ref[...], pm_ref[...])
         + g[:, D_MODEL:] * _dot(yd_ref[...], pd_ref[...]))
    o_ref[...] = x_ref[...] + _dot(m.astype(BF16), wo_ref[...])


def _merge(x1, ym, yd, g, pm, pd, wo, *, tm):
    n = x1.shape[0]

    def row(w):
        return pl.BlockSpec((tm, w), lambda i: (i, 0))

    return pl.pallas_call(
        _merge_kernel,
        grid=(n // tm,),
        in_specs=[row(D_MODEL), row(ym.shape[1]), row(yd.shape[1]), row(g.shape[1]),
                  _resident(pm.shape), _resident(pd.shape), _resident(wo.shape)],
        out_specs=row(D_MODEL),
        out_shape=jax.ShapeDtypeStruct((n, D_MODEL), F32),
        compiler_params=pltpu.CompilerParams(dimension_semantics=("arbitrary",),
                                             vmem_limit_bytes=VMEM_LIMIT),
        name="merge",
    )(x1, ym, yd, g, pm, pd, wo)


def _ffn_weights(wg, wu, wd):
    def cols(w):
        return w.reshape(D_MODEL, N_FF_CHUNKS, FF_CHUNK).transpose(1, 0, 2).astype(BF16)
    return cols(wg), cols(wu), wd.reshape(N_FF_CHUNKS, FF_CHUNK, D_MODEL).astype(BF16)


def _swap_halves(w):
    half = w.shape[-1] // 2
    return jnp.concatenate([w[..., half:], w[..., :half]], axis=-1)


def _inproj_weights(w_in, w_uq, w_ukv):
    d = w_in.shape[0]
    cuts = np.cumsum((MLA_Q_RANK, MLA_KV_RANK, MLA_ROPE, DIFF_WIDTH, DIFF_WIDTH, DIFF_WIDTH, D_MODEL))
    c_q, c_kv, kr, dq, dk, dv, gates = (w_in[:, a:b] for a, b in
                                        zip((0, *cuts[:-1]), cuts))
    gates = jnp.concatenate([gates, w_in[:, cuts[-1]:]], axis=1)

    def slot(w):
        return jnp.pad(w, ((0, 0), (MLA_NOPE, LANES - MLA_NOPE - MLA_ROPE)))

    def diff_swapped(w):
        w3 = w.reshape(d, 2 * DIFF_HEADS, DIFF_HEAD_DIM)
        sw = _swap_halves(w3[..., :DIFF_ROT])
        return jnp.pad(sw, ((0, 0), (0, 0), (0, DIFF_HEAD_DIM - DIFF_ROT))).reshape(d, DIFF_WIDTH)

    dscale = DIFF_HEAD_DIM ** -0.5
    win = jnp.concatenate([c_q, c_kv, slot(kr), slot(_swap_halves(kr)),
                           dq * dscale, diff_swapped(dq) * dscale, dk, diff_swapped(dk), dv, gates],
                          axis=1).astype(BF16)

    qscale = (MLA_NOPE + MLA_ROPE) ** -0.5
    uq = w_uq.reshape(MLA_Q_RANK, MLA_HEADS, MLA_NOPE + MLA_ROPE) * qscale
    pad_r = LANES - MLA_NOPE - MLA_ROPE
    wuqa = jnp.pad(uq, ((0, 0), (0, 0), (0, pad_r)))
    wuqb = jnp.pad(_swap_halves(uq[..., MLA_NOPE:]), ((0, 0), (0, 0), (MLA_NOPE, pad_r)))
    ukv = w_ukv.reshape(MLA_KV_RANK, MLA_HEADS, MLA_NOPE + MLA_V)
    wuk = jnp.pad(ukv[..., :MLA_NOPE], ((0, 0), (0, 0), (0, LANES - MLA_NOPE)))
    wuv = jnp.pad(ukv[..., MLA_NOPE:], ((0, 0), (0, 0), (0, LANES - MLA_V)))
    flat = lambda w: w.reshape(w.shape[0], MLA_HEADS * LANES).astype(BF16)
    return win, flat(wuqa), flat(wuqb), flat(wuk), flat(wuv)


def _rotary_rows():
    def inv_freq(rot_dim):
        half = rot_dim // 2
        return np.power(np.float32(ROPE_THETA), -2.0 * np.arange(half, dtype=np.float32) / rot_dim)

    rows = np.zeros((8, LANES), np.float32)
    hm = MLA_ROPE // 2
    fm = inv_freq(MLA_ROPE)
    rows[0, MLA_NOPE:MLA_NOPE + hm] = fm
    rows[0, MLA_NOPE + hm:MLA_NOPE + MLA_ROPE] = fm
    rows[1, MLA_NOPE:MLA_NOPE + hm] = -1.0
    rows[1, MLA_NOPE + hm:MLA_NOPE + MLA_ROPE] = 1.0
    hd = DIFF_ROT // 2
    fd = inv_freq(DIFF_ROT)
    for base in (0, DIFF_HEAD_DIM):
        rows[2, base:base + hd] = fd
        rows[2, base + hd:base + DIFF_ROT] = fd
        rows[3, base:base + hd] = -1.0
        rows[3, base + hd:base + DIFF_ROT] = 1.0
    rows[4, :DIFF_HEAD_DIM] = 1.0
    rows[5, DIFF_HEAD_DIM:] = 1.0
    return jnp.asarray(rows)


def kernel(x, positions, ffn1_norm, ffn1_w_gate, ffn1_w_up, ffn1_w_down, mix_norm, w_in, mla_q_norm, mla_w_uq, mla_kv_norm, mla_w_ukv, diff_lam_q1, diff_lam_k1, diff_lam_q2, diff_lam_k2, diff_subln, w_proj_mla, w_proj_diff, w_out, ffn2_norm, ffn2_w_gate, ffn2_w_up, ffn2_w_down, final_norm):
    depth = ffn1_norm.shape[0]
    b, s, d = x.shape
    n = b * s
    x2d = x.reshape(n, d)
    posb = jnp.broadcast_to(positions.astype(F32).reshape(n, 1), (n, LANES))
    rot = _rotary_rows()
    final_w = final_norm.reshape(1, d)

    for l in range(depth):
        lam_init = 0.8 - 0.6 * math.exp(-0.3 * l)
        f1 = _ffn_weights(ffn1_w_gate[l], ffn1_w_up[l], ffn1_w_down[l])
        f2 = _ffn_weights(ffn2_w_gate[l], ffn2_w_up[l], ffn2_w_down[l])
        win, wuqa, wuqb, wuk, wuv = _inproj_weights(w_in[l], mla_w_uq[l], mla_w_ukv[l])
        pm = jnp.pad(w_proj_mla[l].reshape(MLA_HEADS, MLA_V, d),
                     ((0, 0), (0, LANES - MLA_V), (0, 0))).reshape(MLA_HEADS * LANES, d).astype(BF16)
        pd = w_proj_diff[l].astype(BF16)
        wo = w_out[l].astype(BF16)
        lam_rows = jnp.stack([diff_lam_q1[l], diff_lam_k1[l], diff_lam_q2[l], diff_lam_k2[l]])

        x1 = _ffn(x2d, ffn1_norm[l].reshape(1, d), *f1, final_w, final=False, tm=512)
        qm, km, vm, qd1, qd2, kd, vd, g = _inproj(
            x1, posb, mix_norm[l].reshape(1, d), win, mla_q_norm[l].reshape(1, -1), wuqa, wuqb,
            mla_kv_norm[l].reshape(1, -1), wuk, wuv, rot, tm=512)

        def r3(t):
            return t.reshape(b, s, t.shape[1])

        ym = _flash([r3(qm)], r3(km), r3(vm), [], heads=MLA_HEADS, tq=512, tk=512,
                    finish=_mla_finish, name="mla_attn")
        yd = _flash([r3(qd1), r3(qd2)], r3(kd), r3(vd), [lam_rows, diff_subln[l].reshape(1, -1)],
                    heads=DIFF_HEADS, tq=512, tk=512,
                    finish=functools.partial(_diff_finish, lam_init=lam_init), name="diff_attn")
        x2 = _merge(x1, ym.reshape(n, -1), yd.reshape(n, -1), g, pm, pd, wo, tm=512)
        last = l == depth - 1
        x2d = _ffn(x2, ffn2_norm[l].reshape(1, d), *f2, final_w, final=last, tm=512)
    return x2d.reshape(b, s, d)
```

```python
import functools
import math

import numpy as np
import jax
import jax.numpy as jnp
from jax import lax
from jax.experimental import pallas as pl
from jax.experimental.pallas import tpu as pltpu

F32 = jnp.float32
BF16 = jnp.bfloat16

D_MODEL = 1024
D_FF = 2816
ROPE_THETA = 500000.0
EPS = 1e-6
LOG2_E = math.log2(math.e)

MLA_HEADS = 8
MLA_Q_RANK = 256
MLA_KV_RANK = 128
MLA_NOPE = 64
MLA_ROPE = 32
MLA_V = 64

DIFF_HEADS = 4
DIFF_HEAD_DIM = 64
DIFF_ROT = DIFF_HEAD_DIM // 4
DIFF_WIDTH = DIFF_HEADS * 2 * DIFF_HEAD_DIM

LANES = 128
FF_CHUNK = 256
N_FF_CHUNKS = D_FF // FF_CHUNK
VMEM_LIMIT = 56 * 1024 * 1024

C_CQ = 0
C_CKV = C_CQ + MLA_Q_RANK
C_KRA = C_CKV + MLA_KV_RANK
C_KRB = C_KRA + LANES
C_DQ = C_KRB + LANES
C_DQS = C_DQ + DIFF_WIDTH
C_DK = C_DQS + DIFF_WIDTH
C_DKS = C_DK + DIFF_WIDTH
C_DV = C_DKS + DIFF_WIDTH
C_GATE = C_DV + DIFF_WIDTH
N_IN_COLS = C_GATE + 2 * D_MODEL


def _rms(xf, w):
    ms = jnp.mean(xf * xf, axis=-1, keepdims=True)
    return xf * lax.rsqrt(ms + EPS) * w


def _dot(a, b):
    return jnp.dot(a, b, preferred_element_type=F32)


def _resident(shape):
    nd = len(shape)
    return pl.BlockSpec(shape, lambda *_: (0,) * nd, pipeline_mode=pl.Buffered(1))


def _ffn_kernel(x_ref, nw_ref, wg_ref, wu_ref, wd_ref, fw_ref, o_ref, xn_ref, acc_ref, *, final):
    x = x_ref[...]
    xn_ref[...] = _rms(x, nw_ref[...]).astype(BF16)
    acc_ref[...] = jnp.zeros_like(acc_ref)

    def body(c, carry):
        xn = xn_ref[...]
        g = _dot(xn, wg_ref[c])
        u = _dot(xn, wu_ref[c])
        a = (g * jax.nn.sigmoid(g) * u).astype(BF16)
        acc_ref[...] += _dot(a, wd_ref[c])
        return carry

    lax.fori_loop(0, N_FF_CHUNKS, body, 0)
    y = x + 0.5 * acc_ref[...]
    if final:
        y = _rms(y, fw_ref[...])
    o_ref[...] = y


def _ffn(x2d, norm_w, wg, wu, wd, final_w, *, final, tm):
    n = x2d.shape[0]
    row = pl.BlockSpec((tm, D_MODEL), lambda i: (i, 0))
    return pl.pallas_call(
        functools.partial(_ffn_kernel, final=final),
        grid=(n // tm,),
        in_specs=[row, _resident((1, D_MODEL)), _resident(wg.shape), _resident(wu.shape),
                  _resident(wd.shape), _resident((1, D_MODEL))],
        out_specs=row,
        out_shape=jax.ShapeDtypeStruct((n, D_MODEL), F32),
        scratch_shapes=[pltpu.VMEM((tm, D_MODEL), BF16), pltpu.VMEM((tm, D_MODEL), F32)],
        compiler_params=pltpu.CompilerParams(dimension_semantics=("arbitrary",),
                                             vmem_limit_bytes=VMEM_LIMIT),
        name="ffn_final" if final else "ffn",
    )(x2d, norm_w, wg, wu, wd, final_w)


def _inproj_kernel(x_ref, pos_ref, mixw_ref, win_ref, qnw_ref, wuqa_ref, wuqb_ref, kvnw_ref,
                   wuk_ref, wuv_ref, rot_ref,
                   qm_ref, km_ref, vm_ref, qd1_ref, qd2_ref, kd_ref, vd_ref, g_ref):
    h = _rms(x_ref[...], mixw_ref[...]).astype(BF16)
    pos = pos_ref[...]
    rot = rot_ref[...]
    ang_m = pos * rot[0:1]
    cos_m = jnp.cos(ang_m)
    sin_m = jnp.sin(ang_m) * rot[1:2]
    ang_d = pos * rot[2:3]
    cos_d = jnp.cos(ang_d)
    sin_d = jnp.sin(ang_d) * rot[3:4]
    lo, hi = rot[4:5], rot[5:6]

    z0 = _dot(h, win_ref[:, C_CQ:C_DQ])
    cqn = _rms(z0[:, C_CQ:C_CKV], qnw_ref[...]).astype(BF16)
    kvn = _rms(z0[:, C_CKV:C_KRA], kvnw_ref[...]).astype(BF16)
    k_rope = z0[:, C_KRA:C_KRB] * cos_m + z0[:, C_KRB:C_DQ] * sin_m

    qa = _dot(cqn, wuqa_ref[...])
    qb = _dot(cqn, wuqb_ref[...])
    ka = _dot(kvn, wuk_ref[...])
    for hd in range(MLA_HEADS):
        sl = slice(hd * LANES, (hd + 1) * LANES)
        qm_ref[:, sl] = (qa[:, sl] * cos_m + qb[:, sl] * sin_m).astype(BF16)
        km_ref[:, sl] = (ka[:, sl] + k_rope).astype(BF16)
    vm_ref[...] = _dot(kvn, wuv_ref[...]).astype(BF16)

    dq = _dot(h, win_ref[:, C_DQ:C_DQS])
    dqs = _dot(h, win_ref[:, C_DQS:C_DK])
    dk = _dot(h, win_ref[:, C_DK:C_DKS])
    dks = _dot(h, win_ref[:, C_DKS:C_DV])
    for hd in range(DIFF_HEADS):
        sl = slice(hd * LANES, (hd + 1) * LANES)
        qr = dq[:, sl] * cos_d + dqs[:, sl] * sin_d
        qd1_ref[:, sl] = (qr * lo).astype(BF16)
        qd2_ref[:, sl] = (qr * hi).astype(BF16)
        kd_ref[:, sl] = (dk[:, sl] * cos_d + dks[:, sl] * sin_d).astype(BF16)
    vd_ref[...] = _dot(h, win_ref[:, C_DV:C_GATE]).astype(BF16)
    g_ref[...] = jax.nn.sigmoid(_dot(h, win_ref[:, C_GATE:N_IN_COLS])).astype(BF16)


def _inproj(x1, posb, mixw, win, qnw, wuqa, wuqb, kvnw, wuk, wuv, rot, *, tm):
    n = x1.shape[0]

    def row(w):
        return pl.BlockSpec((tm, w), lambda i: (i, 0))

    wide = MLA_HEADS * LANES
    out_widths = (wide, wide, wide, DIFF_WIDTH, DIFF_WIDTH, DIFF_WIDTH, DIFF_WIDTH, 2 * D_MODEL)
    return pl.pallas_call(
        _inproj_kernel,
        grid=(n // tm,),
        in_specs=[row(D_MODEL), row(LANES), _resident(mixw.shape), _resident(win.shape),
                  _resident(qnw.shape), _resident(wuqa.shape), _resident(wuqb.shape),
                  _resident(kvnw.shape), _resident(wuk.shape), _resident(wuv.shape),
                  _resident(rot.shape)],
        out_specs=[row(w) for w in out_widths],
        out_shape=[jax.ShapeDtypeStruct((n, w), BF16) for w in out_widths],
        compiler_params=pltpu.CompilerParams(dimension_semantics=("arbitrary",),
                                             vmem_limit_bytes=VMEM_LIMIT),
        name="inproj",
    )(x1, posb, mixw, win, qnw, wuqa, wuqb, kvnw, wuk, wuv, rot)


def _scores(q, k):
    return lax.dot_general(q, k, (((1,), (1,)), ((), ())), preferred_element_type=F32)


def _softmax_update(s, v, m_ref, l_ref, acc_ref, masked):
    rows, tk = s.shape
    if masked:
        r = lax.broadcasted_iota(jnp.int32, (rows, tk), 0)
        c = lax.broadcasted_iota(jnp.int32, (rows, tk), 1)
        s = jnp.where(c <= r, s, -jnp.inf)
    m_prev = m_ref[...]
    m_new = jnp.maximum(m_prev, jnp.max(s, axis=1, keepdims=True))
    alpha = jnp.exp2(m_prev - m_new)
    p_tiles = [jnp.exp2(s[:, c * LANES:(c + 1) * LANES] - m_new) for c in range(tk // LANES)]
    l_ref[...] = alpha * l_ref[...] + functools.reduce(lambda a, b: a + b, p_tiles)
    p = jnp.concatenate(p_tiles, axis=1).astype(BF16)
    acc_ref[...] = alpha * acc_ref[...] + _dot(p, v)
    m_ref[...] = m_new


def _flash_kernel(*refs, n_q, splits, rows, finish):
    assert splits % 2 == 0
    n_chains = n_q * splits
    per_chain = 5
    q_refs = refs[:n_q]
    k_ref, v_ref = refs[n_q], refs[n_q + 1]
    n_scratch = per_chain * n_chains
    extra = refs[n_q + 2:len(refs) - 1 - n_scratch]
    o_ref = refs[len(refs) - 1 - n_scratch]
    scratch = refs[len(refs) - n_scratch:]
    chains = []
    for i in range(n_q):
        for c in range(splits):
            j = per_chain * (i * splits + c)
            chains.append((q_refs[i][0, c * rows:(c + 1) * rows, :], c, *scratch[j:j + 3], scratch[j + 3:j + 5]))
    for _, _, m_ref, l_ref, acc_ref, _ in chains:
        m_ref[...] = jnp.full_like(m_ref, -jnp.inf)
        l_ref[...] = jnp.zeros_like(l_ref)
        acc_ref[...] = jnp.zeros_like(acc_ref)

    base = pl.program_id(2) * splits

    def tile(ref, j):
        return ref[0, pl.ds(pl.multiple_of(j * rows, rows), rows), :]

    def scores(j, slot, diag=None):
        k = tile(k_ref, j)
        for q, off, _, _, _, s_refs in chains:
            if diag is None or off >= diag:
                s_refs[slot][...] = _scores(q, k)

    def update(j, slot, diag=None):
        v = tile(v_ref, j)
        for _, off, m_ref, l_ref, acc_ref, s_refs in chains:
            if diag is None or off >= diag:
                _softmax_update(s_refs[slot][...], v, m_ref, l_ref, acc_ref, masked=(off == diag))

    def two_tiles(jj, carry):
        j = 2 * jj
        scores(j + 1, 1)
        update(j, 0)
        scores(j + 2, 0)
        update(j + 1, 1)
        return carry

    scores(0, 0)
    lax.fori_loop(0, base // 2, two_tiles, 0)
    for d in range(splits):
        if d + 1 < splits:
            scores(base + d + 1, (d + 1) % 2, diag=d + 1)
        update(base + d, d % 2, diag=d)

    outs = [acc_ref[...] / jnp.sum(l_ref[...], axis=1, keepdims=True) for _, _, _, l_ref, acc_ref, _ in chains]
    finish(outs, o_ref, *extra)


def _flash(qs, k, v, extra, *, heads, splits, rows, finish, name):
    b, s, _ = k.shape
    n_q = len(qs)
    tq = splits * rows
    q_spec = pl.BlockSpec((1, tq, LANES), lambda bi, hi, qi: (bi, qi, hi))
    kv_spec = pl.BlockSpec((1, s, LANES), lambda bi, hi, qi: (bi, 0, hi))
    return pl.pallas_call(
        functools.partial(_flash_kernel, n_q=n_q, splits=splits, rows=rows, finish=finish),
        grid=(b, heads, s // tq),
        in_specs=[q_spec] * n_q + [kv_spec, kv_spec] + [_resident(e.shape) for e in extra],
        out_specs=q_spec,
        out_shape=jax.ShapeDtypeStruct((b, s, heads * LANES), BF16),
        scratch_shapes=([pltpu.VMEM((rows, LANES), F32)] * 3 + [pltpu.VMEM((rows, rows), F32)] * 2) * (n_q * splits),
        compiler_params=pltpu.CompilerParams(
            dimension_semantics=("arbitrary", "arbitrary", "arbitrary"),
            vmem_limit_bytes=VMEM_LIMIT),
        name=name,
    )(*qs, k, v, *extra)


def _mla_finish(outs, o_ref):
    rows = outs[0].shape[0]
    for c, o in enumerate(outs):
        o_ref[0, c * rows:(c + 1) * rows, :] = o.astype(o_ref.dtype)


def _diff_finish(outs, o_ref, lam_ref, subln_ref, *, lam_init):
    lam_v = lam_ref[...]
    e1 = jnp.exp(jnp.sum(lam_v[0:1] * lam_v[1:2], axis=1, keepdims=True))
    e2 = jnp.exp(jnp.sum(lam_v[2:3] * lam_v[3:4], axis=1, keepdims=True))
    lam = e1 - e2 + lam_init
    splits = len(outs) // 2
    rows = outs[0].shape[0]
    for c in range(splits):
        o = outs[c] - lam * outs[splits + c]
        o_ref[0, c * rows:(c + 1) * rows, :] = (_rms(o, subln_ref[...]) * (1.0 - lam_init)).astype(o_ref.dtype)


def _merge_kernel(x_ref, ym_ref, yd_ref, g_ref, pm_ref, pd_ref, wo_ref, o_ref):
    g = g_ref[...]
    m = (g[:, :D_MODEL] * _dot(ym_ref[...], pm_ref[...])
         + g[:, D_MODEL:] * _dot(yd_ref[...], pd_ref[...]))
    o_ref[...] = x_ref[...] + _dot(m.astype(BF16), wo_ref[...])


def _merge(x1, ym, yd, g, pm, pd, wo, *, tm):
    n = x1.shape[0]

    def row(w):
        return pl.BlockSpec((tm, w), lambda i: (i, 0))

    return pl.pallas_call(
        _merge_kernel,
        grid=(n // tm,),
        in_specs=[row(D_MODEL), row(ym.shape[1]), row(yd.shape[1]), row(g.shape[1]),
                  _resident(pm.shape), _resident(pd.shape), _resident(wo.shape)],
        out_specs=row(D_MODEL),
        out_shape=jax.ShapeDtypeStruct((n, D_MODEL), F32),
        compiler_params=pltpu.CompilerParams(dimension_semantics=("arbitrary",),
                                             vmem_limit_bytes=VMEM_LIMIT),
        name="merge",
    )(x1, ym, yd, g, pm, pd, wo)


def _ffn_weights(wg, wu, wd):
    def cols(w):
        return w.reshape(D_MODEL, N_FF_CHUNKS, FF_CHUNK).transpose(1, 0, 2).astype(BF16)
    return cols(wg), cols(wu), wd.reshape(N_FF_CHUNKS, FF_CHUNK, D_MODEL).astype(BF16)


def _swap_halves(w):
    half = w.shape[-1] // 2
    return jnp.concatenate([w[..., half:], w[..., :half]], axis=-1)


def _inproj_weights(w_in, w_uq, w_ukv):
    d = w_in.shape[0]
    cuts = np.cumsum((MLA_Q_RANK, MLA_KV_RANK, MLA_ROPE, DIFF_WIDTH, DIFF_WIDTH, DIFF_WIDTH, D_MODEL))
    c_q, c_kv, kr, dq, dk, dv, gates = (w_in[:, a:b] for a, b in
                                        zip((0, *cuts[:-1]), cuts))
    gates = jnp.concatenate([gates, w_in[:, cuts[-1]:]], axis=1)

    def slot(w):
        return jnp.pad(w, ((0, 0), (MLA_NOPE, LANES - MLA_NOPE - MLA_ROPE)))

    def diff_swapped(w):
        w3 = w.reshape(d, 2 * DIFF_HEADS, DIFF_HEAD_DIM)
        sw = _swap_halves(w3[..., :DIFF_ROT])
        return jnp.pad(sw, ((0, 0), (0, 0), (0, DIFF_HEAD_DIM - DIFF_ROT))).reshape(d, DIFF_WIDTH)

    dscale = DIFF_HEAD_DIM ** -0.5 * LOG2_E
    win = jnp.concatenate([c_q, c_kv, slot(kr), slot(_swap_halves(kr)),
                           dq * dscale, diff_swapped(dq) * dscale, dk, diff_swapped(dk), dv, gates],
                          axis=1).astype(BF16)

    qscale = (MLA_NOPE + MLA_ROPE) ** -0.5 * LOG2_E
    uq = w_uq.reshape(MLA_Q_RANK, MLA_HEADS, MLA_NOPE + MLA_ROPE) * qscale
    pad_r = LANES - MLA_NOPE - MLA_ROPE
    wuqa = jnp.pad(uq, ((0, 0), (0, 0), (0, pad_r)))
    wuqb = jnp.pad(_swap_halves(uq[..., MLA_NOPE:]), ((0, 0), (0, 0), (MLA_NOPE, pad_r)))
    ukv = w_ukv.reshape(MLA_KV_RANK, MLA_HEADS, MLA_NOPE + MLA_V)
    wuk = jnp.pad(ukv[..., :MLA_NOPE], ((0, 0), (0, 0), (0, LANES - MLA_NOPE)))
    wuv = jnp.pad(ukv[..., MLA_NOPE:], ((0, 0), (0, 0), (0, LANES - MLA_V)))
    flat = lambda w: w.reshape(w.shape[0], MLA_HEADS * LANES).astype(BF16)
    return win, flat(wuqa), flat(wuqb), flat(wuk), flat(wuv)


def _rotary_rows():
    def inv_freq(rot_dim):
        half = rot_dim // 2
        return np.power(np.float32(ROPE_THETA), -2.0 * np.arange(half, dtype=np.float32) / rot_dim)

    rows = np.zeros((8, LANES), np.float32)
    hm = MLA_ROPE // 2
    fm = inv_freq(MLA_ROPE)
    rows[0, MLA_NOPE:MLA_NOPE + hm] = fm
    rows[0, MLA_NOPE + hm:MLA_NOPE + MLA_ROPE] = fm
    rows[1, MLA_NOPE:MLA_NOPE + hm] = -1.0
    rows[1, MLA_NOPE + hm:MLA_NOPE + MLA_ROPE] = 1.0
    hd = DIFF_ROT // 2
    fd = inv_freq(DIFF_ROT)
    for base in (0, DIFF_HEAD_DIM):
        rows[2, base:base + hd] = fd
        rows[2, base + hd:base + DIFF_ROT] = fd
        rows[3, base:base + hd] = -1.0
        rows[3, base + hd:base + DIFF_ROT] = 1.0
    rows[4, :DIFF_HEAD_DIM] = 1.0
    rows[5, DIFF_HEAD_DIM:] = 1.0
    return jnp.asarray(rows)


def kernel(x, positions, ffn1_norm, ffn1_w_gate, ffn1_w_up, ffn1_w_down, mix_norm, w_in, mla_q_norm, mla_w_uq, mla_kv_norm, mla_w_ukv, diff_lam_q1, diff_lam_k1, diff_lam_q2, diff_lam_k2, diff_subln, w_proj_mla, w_proj_diff, w_out, ffn2_norm, ffn2_w_gate, ffn2_w_up, ffn2_w_down, final_norm):
    depth = ffn1_norm.shape[0]
    b, s, d = x.shape
    n = b * s
    x2d = x.reshape(n, d)
    posb = jnp.broadcast_to(positions.astype(F32).reshape(n, 1), (n, LANES))
    rot = _rotary_rows()
    final_w = final_norm.reshape(1, d)

    for l in range(depth):
        lam_init = 0.8 - 0.6 * math.exp(-0.3 * l)
        f1 = _ffn_weights(ffn1_w_gate[l], ffn1_w_up[l], ffn1_w_down[l])
        f2 = _ffn_weights(ffn2_w_gate[l], ffn2_w_up[l], ffn2_w_down[l])
        win, wuqa, wuqb, wuk, wuv = _inproj_weights(w_in[l], mla_w_uq[l], mla_w_ukv[l])
        pm = jnp.pad(w_proj_mla[l].reshape(MLA_HEADS, MLA_V, d),
                     ((0, 0), (0, LANES - MLA_V), (0, 0))).reshape(MLA_HEADS * LANES, d).astype(BF16)
        pd = w_proj_diff[l].astype(BF16)
        wo = w_out[l].astype(BF16)
        lam_rows = jnp.stack([diff_lam_q1[l], diff_lam_k1[l], diff_lam_q2[l], diff_lam_k2[l]])

        x1 = _ffn(x2d, ffn1_norm[l].reshape(1, d), *f1, final_w, final=False, tm=512)
        qm, km, vm, qd1, qd2, kd, vd, g = _inproj(
            x1, posb, mix_norm[l].reshape(1, d), win, mla_q_norm[l].reshape(1, -1), wuqa, wuqb,
            mla_kv_norm[l].reshape(1, -1), wuk, wuv, rot, tm=512)

        def r3(t):
            return t.reshape(b, s, t.shape[1])

        ym = _flash([r3(qm)], r3(km), r3(vm), [], heads=MLA_HEADS, splits=2, rows=512,
                    finish=_mla_finish, name="mla_attn")
        yd = _flash([r3(qd1), r3(qd2)], r3(kd), r3(vd), [lam_rows, diff_subln[l].reshape(1, -1)],
                    heads=DIFF_HEADS, splits=2, rows=512,
                    finish=functools.partial(_diff_finish, lam_init=lam_init), name="diff_attn")
        x2 = _merge(x1, ym.reshape(n, -1), yd.reshape(n, -1), g, pm, pd, wo, tm=512)
        last = l == depth - 1
        x2d = _ffn(x2, ffn2_norm[l].reshape(1, d), *f2, final_w, final=last, tm=512)
    return x2d.reshape(b, s, d)
```

```python
import functools
import math

import numpy as np
import jax
import jax.numpy as jnp
from jax import lax
from jax.experimental import pallas as pl
from jax.experimental.pallas import tpu as pltpu

F32 = jnp.float32
BF16 = jnp.bfloat16

D_MODEL = 1024
D_FF = 2816
ROPE_THETA = 500000.0
EPS = 1e-6
LOG2_E = math.log2(math.e)

MLA_HEADS = 8
MLA_Q_RANK = 256
MLA_KV_RANK = 128
MLA_NOPE = 64
MLA_ROPE = 32
MLA_V = 64

DIFF_HEADS = 4
DIFF_HEAD_DIM = 64
DIFF_ROT = DIFF_HEAD_DIM // 4
DIFF_WIDTH = DIFF_HEADS * 2 * DIFF_HEAD_DIM

LANES = 128
FF_CHUNK = 256
N_FF_CHUNKS = D_FF // FF_CHUNK
VMEM_LIMIT = 56 * 1024 * 1024

C_CQ = 0
C_CKV = C_CQ + MLA_Q_RANK
C_KRA = C_CKV + MLA_KV_RANK
C_KRB = C_KRA + LANES
C_DQ = C_KRB + LANES
C_DK = C_DQ + DIFF_WIDTH
C_DV = C_DK + DIFF_WIDTH
C_GATE = C_DV + DIFF_WIDTH
N_IN_COLS = C_GATE + 2 * D_MODEL

(R_FREQ, R_MLA, R_MLA_SIGN, R_ROT_LO, R_ROT_HI, R_UP, R_DN, R_LO, R_HI) = range(9)
N_ROT_ROWS = 16


def _rms(xf, w):
    ms = jnp.mean(xf * xf, axis=-1, keepdims=True)
    return xf * lax.rsqrt(ms + EPS) * w


def _dot(a, b):
    return jnp.dot(a, b, preferred_element_type=F32)


def _resident(shape):
    nd = len(shape)
    return pl.BlockSpec(shape, lambda *_: (0,) * nd, pipeline_mode=pl.Buffered(1))


def _ffn_kernel(x_ref, nw_ref, wg_ref, wu_ref, wd_ref, fw_ref, o_ref, *, final):
    x = x_ref[...]
    xn = _rms(x, nw_ref[...]).astype(BF16)
    acc = None
    for c in range(N_FF_CHUNKS):
        g = _dot(xn, wg_ref[c])
        u = _dot(xn, wu_ref[c])
        a = (g * jax.nn.sigmoid(g) * u).astype(BF16)
        d = _dot(a, wd_ref[c])
        acc = d if acc is None else acc + d
    y = x + 0.5 * acc
    if final:
        y = _rms(y, fw_ref[...])
    o_ref[...] = y


def _ffn(x2d, norm_w, wg, wu, wd, final_w, *, final, tm):
    n = x2d.shape[0]
    row = pl.BlockSpec((tm, D_MODEL), lambda i: (i, 0))
    return pl.pallas_call(
        functools.partial(_ffn_kernel, final=final),
        grid=(n // tm,),
        in_specs=[row, _resident((1, D_MODEL)), _resident(wg.shape), _resident(wu.shape),
                  _resident(wd.shape), _resident((1, D_MODEL))],
        out_specs=row,
        out_shape=jax.ShapeDtypeStruct((n, D_MODEL), F32),
        compiler_params=pltpu.CompilerParams(dimension_semantics=("arbitrary",),
                                             vmem_limit_bytes=VMEM_LIMIT),
        name="ffn_final" if final else "ffn",
    )(x2d, norm_w, wg, wu, wd, final_w)


def _inproj_kernel(x_ref, pos_ref, mixw_ref, win_ref, qnw_ref, wuqa_ref, wuqb_ref, kvnw_ref,
                   wuk_ref, wuv_ref, rot_ref,
                   qm_ref, km_ref, vm_ref, qd1_ref, qd2_ref, kd_ref, vd_ref, g_ref):
    h = _rms(x_ref[...], mixw_ref[...]).astype(BF16)
    rot = rot_ref[...]
    ang = pos_ref[...] * rot[R_FREQ:R_FREQ + 1]
    cos_a, sin_a = jnp.cos(ang), jnp.sin(ang)
    cos_m = jnp.where(rot[R_MLA:R_MLA + 1] > 0, cos_a, 1.0)
    sin_m = sin_a * rot[R_MLA_SIGN:R_MLA_SIGN + 1]
    cos_hi = pltpu.roll(cos_a, DIFF_HEAD_DIM, axis=1)
    sin_hi = pltpu.roll(sin_a, DIFF_HEAD_DIM, axis=1)
    lo, hi = rot[R_LO:R_LO + 1], rot[R_HI:R_HI + 1]
    rot_lo, rot_hi = rot[R_ROT_LO:R_ROT_LO + 1], rot[R_ROT_HI:R_ROT_HI + 1]
    cos_d = jnp.where(rot_lo > 0, cos_a, jnp.where(rot_hi > 0, cos_hi, 1.0))
    sin_d = sin_a * rot_lo + sin_hi * rot_hi
    sin_up = sin_d * rot[R_UP:R_UP + 1]
    sin_dn = sin_d * rot[R_DN:R_DN + 1]

    z0 = _dot(h, win_ref[:, C_CQ:C_DQ])
    cqn = _rms(z0[:, C_CQ:C_CKV], qnw_ref[...]).astype(BF16)
    kvn = _rms(z0[:, C_CKV:C_KRA], kvnw_ref[...]).astype(BF16)
    k_rope = z0[:, C_KRA:C_KRB] * cos_m + z0[:, C_KRB:C_DQ] * sin_m

    qa = _dot(cqn, wuqa_ref[...])
    qb = _dot(cqn, wuqb_ref[...])
    ka = _dot(kvn, wuk_ref[...])
    for hd in range(MLA_HEADS):
        sl = slice(hd * LANES, (hd + 1) * LANES)
        qm_ref[:, sl] = (qa[:, sl] * cos_m + qb[:, sl] * sin_m).astype(BF16)
        km_ref[:, sl] = (ka[:, sl] + k_rope).astype(BF16)
    vm_ref[...] = _dot(kvn, wuv_ref[...]).astype(BF16)

    half = DIFF_ROT // 2

    head_lanes = [slice(hd * LANES, (hd + 1) * LANES) for hd in range(DIFF_HEADS)]

    def diff_rope(x):
        up = pltpu.roll(x, DIFF_WIDTH - half, axis=1)
        dn = pltpu.roll(x, half, axis=1)
        return [x[:, sl] * cos_d + up[:, sl] * sin_up + dn[:, sl] * sin_dn for sl in head_lanes]

    dq = diff_rope(_dot(h, win_ref[:, C_DQ:C_DK]))
    dk = diff_rope(_dot(h, win_ref[:, C_DK:C_DV]))
    for sl, qr, kr in zip(head_lanes, dq, dk):
        qd1_ref[:, sl] = (qr * lo).astype(BF16)
        qd2_ref[:, sl] = (qr * hi).astype(BF16)
        kd_ref[:, sl] = kr.astype(BF16)
    vd_ref[...] = _dot(h, win_ref[:, C_DV:C_GATE]).astype(BF16)
    g_ref[...] = jax.nn.sigmoid(_dot(h, win_ref[:, C_GATE:N_IN_COLS])).astype(BF16)


def _inproj(x1, posb, mixw, win, qnw, wuqa, wuqb, kvnw, wuk, wuv, rot, *, tm):
    n = x1.shape[0]

    def row(w):
        return pl.BlockSpec((tm, w), lambda i: (i, 0))

    wide = MLA_HEADS * LANES
    out_widths = (wide, wide, wide, DIFF_WIDTH, DIFF_WIDTH, DIFF_WIDTH, DIFF_WIDTH, 2 * D_MODEL)
    return pl.pallas_call(
        _inproj_kernel,
        grid=(n // tm,),
        in_specs=[row(D_MODEL), row(LANES), _resident(mixw.shape), _resident(win.shape),
                  _resident(qnw.shape), _resident(wuqa.shape), _resident(wuqb.shape),
                  _resident(kvnw.shape), _resident(wuk.shape), _resident(wuv.shape),
                  _resident(rot.shape)],
        out_specs=[row(w) for w in out_widths],
        out_shape=[jax.ShapeDtypeStruct((n, w), BF16) for w in out_widths],
        compiler_params=pltpu.CompilerParams(dimension_semantics=("arbitrary",),
                                             vmem_limit_bytes=VMEM_LIMIT),
        name="inproj",
    )(x1, posb, mixw, win, qnw, wuqa, wuqb, kvnw, wuk, wuv, rot)


def _scores(q, k):
    return lax.dot_general(q, k, (((1,), (1,)), ((), ())), preferred_element_type=F32)


def _softmax_update(s, v, m_ref, l_ref, acc_ref, masked):
    rows, tk = s.shape
    if masked:
        r = lax.broadcasted_iota(jnp.int32, (rows, tk), 0)
        c = lax.broadcasted_iota(jnp.int32, (rows, tk), 1)
        s = jnp.where(c <= r, s, -jnp.inf)
    m_prev = m_ref[...]
    m_new = jnp.maximum(m_prev, jnp.max(s, axis=1, keepdims=True))
    alpha = jnp.exp2(m_prev - m_new)
    p_tiles = [jnp.exp2(s[:, c * LANES:(c + 1) * LANES] - m_new) for c in range(tk // LANES)]
    l_ref[...] = alpha * l_ref[...] + functools.reduce(lambda a, b: a + b, p_tiles)
    p = jnp.concatenate(p_tiles, axis=1).astype(BF16)
    acc_ref[...] = alpha * acc_ref[...] + _dot(p, v)
    m_ref[...] = m_new


def _flash_kernel(*refs, n_q, splits, rows, finish):
    assert splits % 2 == 0
    n_chains = n_q * splits
    per_chain = 5
    q_refs = refs[:n_q]
    k_ref, v_ref = refs[n_q], refs[n_q + 1]
    n_scratch = per_chain * n_chains
    extra = refs[n_q + 2:len(refs) - 1 - n_scratch]
    o_ref = refs[len(refs) - 1 - n_scratch]
    scratch = refs[len(refs) - n_scratch:]
    chains = []
    for i in range(n_q):
        for c in range(splits):
            j = per_chain * (i * splits + c)
            chains.append((q_refs[i][0, c * rows:(c + 1) * rows, :], c, *scratch[j:j + 3], scratch[j + 3:j + 5]))
    for _, _, m_ref, l_ref, acc_ref, _ in chains:
        m_ref[...] = jnp.full_like(m_ref, -jnp.inf)
        l_ref[...] = jnp.zeros_like(l_ref)
        acc_ref[...] = jnp.zeros_like(acc_ref)

    base = pl.program_id(2) * splits

    def tile(ref, j):
        return ref[0, pl.ds(pl.multiple_of(j * rows, rows), rows), :]

    def scores(j, slot, diag=None):
        k = tile(k_ref, j)
        for q, off, _, _, _, s_refs in chains:
            if diag is None or off >= diag:
                s_refs[slot][...] = _scores(q, k)

    def update(j, slot, diag=None):
        v = tile(v_ref, j)
        for _, off, m_ref, l_ref, acc_ref, s_refs in chains:
            if diag is None or off >= diag:
                _softmax_update(s_refs[slot][...], v, m_ref, l_ref, acc_ref, masked=(off == diag))

    def two_tiles(jj, carry):
        j = 2 * jj
        scores(j + 1, 1)
        update(j, 0)
        scores(j + 2, 0)
        update(j + 1, 1)
        return carry

    scores(0, 0)
    lax.fori_loop(0, base // 2, two_tiles, 0)
    for d in range(splits):
        if d + 1 < splits:
            scores(base + d + 1, (d + 1) % 2, diag=d + 1)
        update(base + d, d % 2, diag=d)

    outs = [acc_ref[...] / jnp.sum(l_ref[...], axis=1, keepdims=True) for _, _, _, l_ref, acc_ref, _ in chains]
    finish(outs, o_ref, *extra)


def _flash(qs, k, v, extra, *, heads, splits, rows, finish, name):
    b, s, _ = k.shape
    n_q = len(qs)
    tq = splits * rows
    q_spec = pl.BlockSpec((1, tq, LANES), lambda bi, hi, qi: (bi, qi, hi))
    kv_spec = pl.BlockSpec((1, s, LANES), lambda bi, hi, qi: (bi, 0, hi))
    return pl.pallas_call(
        functools.partial(_flash_kernel, n_q=n_q, splits=splits, rows=rows, finish=finish),
        grid=(b, heads, s // tq),
        in_specs=[q_spec] * n_q + [kv_spec, kv_spec] + [_resident(e.shape) for e in extra],
        out_specs=q_spec,
        out_shape=jax.ShapeDtypeStruct((b, s, heads * LANES), BF16),
        scratch_shapes=([pltpu.VMEM((rows, LANES), F32)] * 3 + [pltpu.VMEM((rows, rows), F32)] * 2) * (n_q * splits),
        compiler_params=pltpu.CompilerParams(
            dimension_semantics=("arbitrary", "arbitrary", "arbitrary"),
            vmem_limit_bytes=VMEM_LIMIT),
        name=name,
    )(*qs, k, v, *extra)


def _mla_finish(outs, o_ref):
    rows = outs[0].shape[0]
    for c, o in enumerate(outs):
        o_ref[0, c * rows:(c + 1) * rows, :] = o.astype(o_ref.dtype)


def _diff_finish(outs, o_ref, lam_ref, subln_ref, *, lam_init):
    lam_v = lam_ref[...]
    e1 = jnp.exp(jnp.sum(lam_v[0:1] * lam_v[1:2], axis=1, keepdims=True))
    e2 = jnp.exp(jnp.sum(lam_v[2:3] * lam_v[3:4], axis=1, keepdims=True))
    lam = e1 - e2 + lam_init
    splits = len(outs) // 2
    rows = outs[0].shape[0]
    for c in range(splits):
        o = outs[c] - lam * outs[splits + c]
        o_ref[0, c * rows:(c + 1) * rows, :] = (_rms(o, subln_ref[...]) * (1.0 - lam_init)).astype(o_ref.dtype)


def _merge_kernel(x_ref, ym_ref, yd_ref, g_ref, pm_ref, pd_ref, wo_ref, o_ref):
    g = g_ref[...]
    m = (g[:, :D_MODEL] * _dot(ym_ref[...], pm_ref[...])
         + g[:, D_MODEL:] * _dot(yd_ref[...], pd_ref[...]))
    o_ref[...] = x_ref[...] + _dot(m.astype(BF16), wo_ref[...])


def _merge(x1, ym, yd, g, pm, pd, wo, *, tm):
    n = x1.shape[0]

    def row(w):
        return pl.BlockSpec((tm, w), lambda i: (i, 0))

    return pl.pallas_call(
        _merge_kernel,
        grid=(n // tm,),
        in_specs=[row(D_MODEL), row(ym.shape[1]), row(yd.shape[1]), row(g.shape[1]),
                  _resident(pm.shape), _resident(pd.shape), _resident(wo.shape)],
        out_specs=row(D_MODEL),
        out_shape=jax.ShapeDtypeStruct((n, D_MODEL), F32),
        compiler_params=pltpu.CompilerParams(dimension_semantics=("arbitrary",),
                                             vmem_limit_bytes=VMEM_LIMIT),
        name="merge",
    )(x1, ym, yd, g, pm, pd, wo)


def _ffn_weights(wg, wu, wd):
    def cols(w):
        return w.reshape(D_MODEL, N_FF_CHUNKS, FF_CHUNK).transpose(1, 0, 2).astype(BF16)
    return cols(wg), cols(wu), wd.reshape(N_FF_CHUNKS, FF_CHUNK, D_MODEL).astype(BF16)


def _swap_halves(w):
    half = w.shape[-1] // 2
    return jnp.concatenate([w[..., half:], w[..., :half]], axis=-1)


def _inproj_weights(w_in, w_uq, w_ukv):
    d = w_in.shape[0]
    cuts = np.cumsum((MLA_Q_RANK, MLA_KV_RANK, MLA_ROPE, DIFF_WIDTH, DIFF_WIDTH, DIFF_WIDTH, D_MODEL))
    c_q, c_kv, kr, dq, dk, dv, gates = (w_in[:, a:b] for a, b in
                                        zip((0, *cuts[:-1]), cuts))
    gates = jnp.concatenate([gates, w_in[:, cuts[-1]:]], axis=1)

    def slot(w):
        return jnp.pad(w, ((0, 0), (MLA_NOPE, LANES - MLA_NOPE - MLA_ROPE)))

    dscale = DIFF_HEAD_DIM ** -0.5 * LOG2_E
    win = jnp.concatenate([c_q, c_kv, slot(kr), slot(_swap_halves(kr)), dq * dscale, dk, dv, gates],
                          axis=1).astype(BF16)

    qscale = (MLA_NOPE + MLA_ROPE) ** -0.5 * LOG2_E
    uq = w_uq.reshape(MLA_Q_RANK, MLA_HEADS, MLA_NOPE + MLA_ROPE) * qscale
    pad_r = LANES - MLA_NOPE - MLA_ROPE
    wuqa = jnp.pad(uq, ((0, 0), (0, 0), (0, pad_r)))
    wuqb = jnp.pad(_swap_halves(uq[..., MLA_NOPE:]), ((0, 0), (0, 0), (MLA_NOPE, pad_r)))
    ukv = w_ukv.reshape(MLA_KV_RANK, MLA_HEADS, MLA_NOPE + MLA_V)
    wuk = jnp.pad(ukv[..., :MLA_NOPE], ((0, 0), (0, 0), (0, LANES - MLA_NOPE)))
    wuv = jnp.pad(ukv[..., MLA_NOPE:], ((0, 0), (0, 0), (0, LANES - MLA_V)))
    flat = lambda w: w.reshape(w.shape[0], MLA_HEADS * LANES).astype(BF16)
    return win, flat(wuqa), flat(wuqb), flat(wuk), flat(wuv)


def _rotary_rows():
    def inv_freq(rot_dim):
        half = rot_dim // 2
        return np.power(np.float32(ROPE_THETA), -2.0 * np.arange(half, dtype=np.float32) / rot_dim)

    rows = np.zeros((N_ROT_ROWS, LANES), np.float32)
    hm = MLA_ROPE // 2
    m0 = MLA_NOPE
    rows[R_FREQ, m0:m0 + MLA_ROPE] = np.tile(inv_freq(MLA_ROPE), 2)
    rows[R_MLA, m0:m0 + MLA_ROPE] = 1.0
    rows[R_MLA_SIGN, m0:m0 + hm] = -1.0
    rows[R_MLA_SIGN, m0 + hm:m0 + MLA_ROPE] = 1.0
    hd = DIFF_ROT // 2
    rows[R_FREQ, :DIFF_ROT] = np.tile(inv_freq(DIFF_ROT), 2)
    rows[R_ROT_LO, :DIFF_ROT] = 1.0
    rows[R_ROT_HI, DIFF_HEAD_DIM:DIFF_HEAD_DIM + DIFF_ROT] = 1.0
    for base in (0, DIFF_HEAD_DIM):
        rows[R_UP, base:base + hd] = -1.0
        rows[R_DN, base + hd:base + DIFF_ROT] = 1.0
    rows[R_LO, :DIFF_HEAD_DIM] = 1.0
    rows[R_HI, DIFF_HEAD_DIM:] = 1.0
    return jnp.asarray(rows)


def kernel(x, positions, ffn1_norm, ffn1_w_gate, ffn1_w_up, ffn1_w_down, mix_norm, w_in, mla_q_norm, mla_w_uq, mla_kv_norm, mla_w_ukv, diff_lam_q1, diff_lam_k1, diff_lam_q2, diff_lam_k2, diff_subln, w_proj_mla, w_proj_diff, w_out, ffn2_norm, ffn2_w_gate, ffn2_w_up, ffn2_w_down, final_norm):
    depth = ffn1_norm.shape[0]
    b, s, d = x.shape
    n = b * s
    x2d = x.reshape(n, d)
    posb = jnp.broadcast_to(positions.astype(F32).reshape(n, 1), (n, LANES))
    rot = _rotary_rows()
    final_w = final_norm.reshape(1, d)

    for l in range(depth):
        lam_init = 0.8 - 0.6 * math.exp(-0.3 * l)
        f1 = _ffn_weights(ffn1_w_gate[l], ffn1_w_up[l], ffn1_w_down[l])
        f2 = _ffn_weights(ffn2_w_gate[l], ffn2_w_up[l], ffn2_w_down[l])
        win, wuqa, wuqb, wuk, wuv = _inproj_weights(w_in[l], mla_w_uq[l], mla_w_ukv[l])
        pm = jnp.pad(w_proj_mla[l].reshape(MLA_HEADS, MLA_V, d),
                     ((0, 0), (0, LANES - MLA_V), (0, 0))).reshape(MLA_HEADS * LANES, d).astype(BF16)
        pd = w_proj_diff[l].astype(BF16)
        wo = w_out[l].astype(BF16)
        lam_rows = jnp.stack([diff_lam_q1[l], diff_lam_k1[l], diff_lam_q2[l], diff_lam_k2[l]])

        x1 = _ffn(x2d, ffn1_norm[l].reshape(1, d), *f1, final_w, final=False, tm=512)
        qm, km, vm, qd1, qd2, kd, vd, g = _inproj(
            x1, posb, mix_norm[l].reshape(1, d), win, mla_q_norm[l].reshape(1, -1), wuqa, wuqb,
            mla_kv_norm[l].reshape(1, -1), wuk, wuv, rot, tm=512)

        def r3(t):
            return t.reshape(b, s, t.shape[1])

        ym = _flash([r3(qm)], r3(km), r3(vm), [], heads=MLA_HEADS, splits=2, rows=512,
                    finish=_mla_finish, name="mla_attn")
        yd = _flash([r3(qd1), r3(qd2)], r3(kd), r3(vd), [lam_rows, diff_subln[l].reshape(1, -1)],
                    heads=DIFF_HEADS, splits=2, rows=512,
                    finish=functools.partial(_diff_finish, lam_init=lam_init), name="diff_attn")
        x2 = _merge(x1, ym.reshape(n, -1), yd.reshape(n, -1), g, pm, pd, wo, tm=512)
        last = l == depth - 1
        x2d = _ffn(x2, ffn2_norm[l].reshape(1, d), *f2, final_w, final=last, tm=512)
    return x2d.reshape(b, s, d)
```

```python
import functools
import math

import numpy as np
import jax
import jax.numpy as jnp
from jax import lax
from jax.experimental import pallas as pl
from jax.experimental.pallas import tpu as pltpu

F32 = jnp.float32
BF16 = jnp.bfloat16

D_MODEL = 1024
D_FF = 2816
ROPE_THETA = 500000.0
EPS = 1e-6
LOG2_E = math.log2(math.e)

MLA_HEADS = 8
MLA_Q_RANK = 256
MLA_KV_RANK = 128
MLA_NOPE = 64
MLA_ROPE = 32
MLA_V = 64

DIFF_HEADS = 4
DIFF_HEAD_DIM = 64
DIFF_ROT = DIFF_HEAD_DIM // 4
DIFF_WIDTH = DIFF_HEADS * 2 * DIFF_HEAD_DIM

LANES = 128
FF_CHUNK = 256
N_FF_CHUNKS = D_FF // FF_CHUNK
VMEM_LIMIT = 56 * 1024 * 1024

C_CQ = 0
C_CKV = C_CQ + MLA_Q_RANK
C_KRA = C_CKV + MLA_KV_RANK
C_KRB = C_KRA + LANES
C_DQ = C_KRB + LANES
C_DK = C_DQ + DIFF_WIDTH
C_DV = C_DK + DIFF_WIDTH
C_GATE = C_DV + DIFF_WIDTH
N_IN_COLS = C_GATE + 2 * D_MODEL

(R_FREQ, R_MLA, R_MLA_SIGN, R_ROT_LO, R_ROT_HI, R_UP, R_DN, R_LO, R_HI) = range(9)
N_ROT_ROWS = 16


def _rms(xf, w):
    ms = jnp.mean(xf * xf, axis=-1, keepdims=True)
    return xf * lax.rsqrt(ms + EPS) * w


def _dot(a, b):
    return jnp.dot(a, b, preferred_element_type=F32)


def _resident(shape):
    nd = len(shape)
    return pl.BlockSpec(shape, lambda *_: (0,) * nd, pipeline_mode=pl.Buffered(1))


def _ffn_kernel(x_ref, nw_ref, wg_ref, wu_ref, wd_ref, fw_ref, o_ref, *, final):
    x = x_ref[...]
    xn = _rms(x, nw_ref[...]).astype(BF16)
    acc = None
    for c in range(N_FF_CHUNKS):
        g = _dot(xn, wg_ref[c])
        u = _dot(xn, wu_ref[c])
        a = (g * jax.nn.sigmoid(g) * u).astype(BF16)
        d = _dot(a, wd_ref[c])
        acc = d if acc is None else acc + d
    y = x + 0.5 * acc
    if final:
        y = _rms(y, fw_ref[...])
    o_ref[...] = y


def _ffn(x2d, norm_w, wg, wu, wd, final_w, *, final, tm):
    n = x2d.shape[0]
    row = pl.BlockSpec((tm, D_MODEL), lambda i: (i, 0))
    return pl.pallas_call(
        functools.partial(_ffn_kernel, final=final),
        grid=(n // tm,),
        in_specs=[row, _resident((1, D_MODEL)), _resident(wg.shape), _resident(wu.shape),
                  _resident(wd.shape), _resident((1, D_MODEL))],
        out_specs=row,
        out_shape=jax.ShapeDtypeStruct((n, D_MODEL), F32),
        compiler_params=pltpu.CompilerParams(dimension_semantics=("arbitrary",),
                                             vmem_limit_bytes=VMEM_LIMIT),
        name="ffn_final" if final else "ffn",
    )(x2d, norm_w, wg, wu, wd, final_w)


def _inproj_kernel(x_ref, pos_ref, mixw_ref, win_ref, qnw_ref, wuqa_ref, wuqb_ref, kvnw_ref,
                   wuk_ref, wuv_ref, rot_ref,
                   qm_ref, km_ref, vm_ref, qd1_ref, qd2_ref, kd_ref, vd_ref, g_ref):
    h = _rms(x_ref[...], mixw_ref[...]).astype(BF16)
    rot = rot_ref[...]
    ang = pos_ref[...] * rot[R_FREQ:R_FREQ + 1]
    cos_a, sin_a = jnp.cos(ang), jnp.sin(ang)
    cos_m = jnp.where(rot[R_MLA:R_MLA + 1] > 0, cos_a, 1.0)
    sin_m = sin_a * rot[R_MLA_SIGN:R_MLA_SIGN + 1]
    cos_hi = pltpu.roll(cos_a, DIFF_HEAD_DIM, axis=1)
    sin_hi = pltpu.roll(sin_a, DIFF_HEAD_DIM, axis=1)
    lo, hi = rot[R_LO:R_LO + 1], rot[R_HI:R_HI + 1]
    rot_lo, rot_hi = rot[R_ROT_LO:R_ROT_LO + 1], rot[R_ROT_HI:R_ROT_HI + 1]
    cos_d = jnp.where(rot_lo > 0, cos_a, jnp.where(rot_hi > 0, cos_hi, 1.0))
    sin_d = sin_a * rot_lo + sin_hi * rot_hi
    sin_up = sin_d * rot[R_UP:R_UP + 1]
    sin_dn = sin_d * rot[R_DN:R_DN + 1]

    z0 = _dot(h, win_ref[:, C_CQ:C_DQ])
    cqn = _rms(z0[:, C_CQ:C_CKV], qnw_ref[...]).astype(BF16)
    kvn = _rms(z0[:, C_CKV:C_KRA], kvnw_ref[...]).astype(BF16)
    k_rope = z0[:, C_KRA:C_KRB] * cos_m + z0[:, C_KRB:C_DQ] * sin_m

    qa = _dot(cqn, wuqa_ref[...])
    qb = _dot(cqn, wuqb_ref[...])
    ka = _dot(kvn, wuk_ref[...])
    for hd in range(MLA_HEADS):
        sl = slice(hd * LANES, (hd + 1) * LANES)
        qm_ref[:, sl] = (qa[:, sl] * cos_m + qb[:, sl] * sin_m).astype(BF16)
        km_ref[:, sl] = (ka[:, sl] + k_rope).astype(BF16)
    vm = _dot(kvn, wuv_ref[...])
    lane = lax.broadcasted_iota(jnp.int32, vm.shape, 1) % LANES
    vm_ref[...] = jnp.where(lane == MLA_V, 1.0, vm).astype(BF16)

    half = DIFF_ROT // 2

    head_lanes = [slice(hd * LANES, (hd + 1) * LANES) for hd in range(DIFF_HEADS)]

    def diff_rope(x):
        up = pltpu.roll(x, DIFF_WIDTH - half, axis=1)
        dn = pltpu.roll(x, half, axis=1)
        return [x[:, sl] * cos_d + up[:, sl] * sin_up + dn[:, sl] * sin_dn for sl in head_lanes]

    dq = diff_rope(_dot(h, win_ref[:, C_DQ:C_DK]))
    dk = diff_rope(_dot(h, win_ref[:, C_DK:C_DV]))
    for sl, qr, kr in zip(head_lanes, dq, dk):
        qd1_ref[:, sl] = (qr * lo).astype(BF16)
        qd2_ref[:, sl] = (qr * hi).astype(BF16)
        kd_ref[:, sl] = kr.astype(BF16)
    vd_ref[...] = _dot(h, win_ref[:, C_DV:C_GATE]).astype(BF16)
    g_ref[...] = jax.nn.sigmoid(_dot(h, win_ref[:, C_GATE:N_IN_COLS])).astype(BF16)


def _inproj(x1, posb, mixw, win, qnw, wuqa, wuqb, kvnw, wuk, wuv, rot, *, tm):
    n = x1.shape[0]

    def row(w):
        return pl.BlockSpec((tm, w), lambda i: (i, 0))

    wide = MLA_HEADS * LANES
    out_widths = (wide, wide, wide, DIFF_WIDTH, DIFF_WIDTH, DIFF_WIDTH, DIFF_WIDTH, 2 * D_MODEL)
    return pl.pallas_call(
        _inproj_kernel,
        grid=(n // tm,),
        in_specs=[row(D_MODEL), row(LANES), _resident(mixw.shape), _resident(win.shape),
                  _resident(qnw.shape), _resident(wuqa.shape), _resident(wuqb.shape),
                  _resident(kvnw.shape), _resident(wuk.shape), _resident(wuv.shape),
                  _resident(rot.shape)],
        out_specs=[row(w) for w in out_widths],
        out_shape=[jax.ShapeDtypeStruct((n, w), BF16) for w in out_widths],
        compiler_params=pltpu.CompilerParams(dimension_semantics=("arbitrary",),
                                             vmem_limit_bytes=VMEM_LIMIT),
        name="inproj",
    )(x1, posb, mixw, win, qnw, wuqa, wuqb, kvnw, wuk, wuv, rot)


class _Chain:
    def __init__(self, q, head, off, refs):
        self.q, self.head, self.off = q, head, off
        self.m, self.l, self.acc = refs[0:3]
        self.s = refs[3:5]


_CHAIN_REFS = 5


def _softmax_update(ch, slot, v, masked, l_row):
    s = ch.s[slot][...]
    tk, rows = s.shape
    if masked:
        kidx = lax.broadcasted_iota(jnp.int32, (tk, rows), 0)
        qidx = lax.broadcasted_iota(jnp.int32, (tk, rows), 1)
        s = jnp.where(kidx <= qidx, s, -jnp.inf)
    m_prev = ch.m[...]
    m_new = jnp.maximum(m_prev, jnp.max(s, axis=0, keepdims=True))
    alpha = jnp.exp2(m_prev - m_new)
    p = jnp.exp2(s - m_new)
    if l_row is None:
        ch.l[...] = alpha * ch.l[...] + jnp.sum(p, axis=0, keepdims=True)
    pv = lax.dot_general(v, p.astype(BF16), (((0,), (0,)), ((), ())), preferred_element_type=F32)
    ch.acc[...] = alpha * ch.acc[...] + pv
    ch.m[...] = m_new


def _flash_kernel(*refs, n_q, heads, splits, rows, l_row, finish):
    assert splits % 2 == 0
    n_chains = n_q * heads * splits
    q_refs = refs[:n_q]
    k_ref, v_ref = refs[n_q], refs[n_q + 1]
    n_scratch = _CHAIN_REFS * n_chains
    extra = refs[n_q + 2:len(refs) - 1 - n_scratch]
    o_ref = refs[len(refs) - 1 - n_scratch]
    scratch = iter(refs[len(refs) - n_scratch:])
    chains = [_Chain(q_refs[i][0, c * rows:(c + 1) * rows, h * LANES:(h + 1) * LANES], h, c,
                     [next(scratch) for _ in range(_CHAIN_REFS)])
              for i in range(n_q) for h in range(heads) for c in range(splits)]
    for ch in chains:
        ch.m[...] = jnp.full_like(ch.m, -jnp.inf)
        ch.l[...] = jnp.zeros_like(ch.l)
        ch.acc[...] = jnp.zeros_like(ch.acc)

    base = pl.program_id(2) * splits

    def tile(ref, j, head):
        return ref[0, pl.ds(pl.multiple_of(j * rows, rows), rows), head * LANES:(head + 1) * LANES]

    def scores(j, slot, diag=None):
        for ch in chains:
            if diag is None or ch.off >= diag:
                ch.s[slot][...] = lax.dot_general(tile(k_ref, j, ch.head), ch.q, (((1,), (1,)), ((), ())),
                                                  preferred_element_type=F32)

    def update(j, slot, diag=None):
        for ch in chains:
            if diag is None or ch.off >= diag:
                _softmax_update(ch, slot, tile(v_ref, j, ch.head), ch.off == diag, l_row)

    def two_tiles(jj, carry):
        j = 2 * jj
        scores(j + 1, 1)
        update(j, 0)
        scores(j + 2, 0)
        update(j + 1, 1)
        return carry

    scores(0, 0)
    lax.fori_loop(0, base // 2, two_tiles, 0)
    for d in range(splits):
        if d + 1 < splits:
            scores(base + d + 1, (d + 1) % 2, diag=d + 1)
        update(base + d, d % 2, diag=d)

    outs = []
    for ch in chains:
        acc = ch.acc[...]
        outs.append(acc / (ch.l[...] if l_row is None else acc[l_row:l_row + 1]))
    finish(outs, o_ref, *extra)


def _flash(qs, k, v, extra, *, groups, heads, splits, rows, l_row, finish, name):
    b, s, _ = k.shape
    n_q = len(qs)
    tq = splits * rows
    q_spec = pl.BlockSpec((1, tq, heads * LANES), lambda bi, gi, qi: (bi, qi, gi))
    kv_spec = pl.BlockSpec((1, s, heads * LANES), lambda bi, gi, qi: (bi, 0, gi))
    chain_scratch = ([pltpu.VMEM((1, rows), F32)] * 2 + [pltpu.VMEM((LANES, rows), F32)]
                     + [pltpu.VMEM((rows, rows), F32)] * 2)
    return pl.pallas_call(
        functools.partial(_flash_kernel, n_q=n_q, heads=heads, splits=splits, rows=rows, l_row=l_row,
                          finish=finish),
        grid=(b, groups, s // tq),
        in_specs=[q_spec] * n_q + [kv_spec, kv_spec] + [_resident(e.shape) for e in extra],
        out_specs=pl.BlockSpec((1, tq, LANES), lambda bi, gi, qi: (bi, qi, gi)),
        out_shape=jax.ShapeDtypeStruct((b, s, groups * LANES), BF16),
        scratch_shapes=chain_scratch * (n_q * heads * splits),
        compiler_params=pltpu.CompilerParams(
            dimension_semantics=("arbitrary", "arbitrary", "arbitrary"),
            vmem_limit_bytes=VMEM_LIMIT),
        name=name,
    )(*qs, k, v, *extra)


def _mla_finish(outs, o_ref, *, splits):
    rows = outs[0].shape[1]
    for c in range(splits):
        pair = jnp.concatenate([outs[c][:MLA_V], outs[splits + c][:MLA_V]], axis=0)
        o_ref[0, c * rows:(c + 1) * rows, :] = pair.T.astype(o_ref.dtype)


def _diff_finish(outs, o_ref, lam_ref, subln_ref, *, lam_init):
    lam_v = lam_ref[...]
    e1 = jnp.exp(jnp.sum(lam_v[0:1] * lam_v[1:2], axis=1, keepdims=True))
    e2 = jnp.exp(jnp.sum(lam_v[2:3] * lam_v[3:4], axis=1, keepdims=True))
    lam = e1 - e2 + lam_init
    splits = len(outs) // 2
    rows = outs[0].shape[1]
    for c in range(splits):
        o = (outs[c] - lam * outs[splits + c]).T
        o_ref[0, c * rows:(c + 1) * rows, :] = (_rms(o, subln_ref[...]) * (1.0 - lam_init)).astype(o_ref.dtype)


def _merge_kernel(x_ref, ym_ref, yd_ref, g_ref, pm_ref, pd_ref, wo_ref, o_ref):
    g = g_ref[...]
    m = (g[:, :D_MODEL] * _dot(ym_ref[...], pm_ref[...])
         + g[:, D_MODEL:] * _dot(yd_ref[...], pd_ref[...]))
    o_ref[...] = x_ref[...] + _dot(m.astype(BF16), wo_ref[...])


def _merge(x1, ym, yd, g, pm, pd, wo, *, tm):
    n = x1.shape[0]

    def row(w):
        return pl.BlockSpec((tm, w), lambda i: (i, 0))

    return pl.pallas_call(
        _merge_kernel,
        grid=(n // tm,),
        in_specs=[row(D_MODEL), row(ym.shape[1]), row(yd.shape[1]), row(g.shape[1]),
                  _resident(pm.shape), _resident(pd.shape), _resident(wo.shape)],
        out_specs=row(D_MODEL),
        out_shape=jax.ShapeDtypeStruct((n, D_MODEL), F32),
        compiler_params=pltpu.CompilerParams(dimension_semantics=("arbitrary",),
                                             vmem_limit_bytes=VMEM_LIMIT),
        name="merge",
    )(x1, ym, yd, g, pm, pd, wo)


def _ffn_weights(wg, wu, wd):
    def cols(w):
        return w.reshape(D_MODEL, N_FF_CHUNKS, FF_CHUNK).transpose(1, 0, 2).astype(BF16)
    return cols(wg), cols(wu), wd.reshape(N_FF_CHUNKS, FF_CHUNK, D_MODEL).astype(BF16)


def _swap_halves(w):
    half = w.shape[-1] // 2
    return jnp.concatenate([w[..., half:], w[..., :half]], axis=-1)


def _inproj_weights(w_in, w_uq, w_ukv):
    d = w_in.shape[0]
    cuts = np.cumsum((MLA_Q_RANK, MLA_KV_RANK, MLA_ROPE, DIFF_WIDTH, DIFF_WIDTH, DIFF_WIDTH, D_MODEL))
    c_q, c_kv, kr, dq, dk, dv, gates = (w_in[:, a:b] for a, b in
                                        zip((0, *cuts[:-1]), cuts))
    gates = jnp.concatenate([gates, w_in[:, cuts[-1]:]], axis=1)

    def slot(w):
        return jnp.pad(w, ((0, 0), (MLA_NOPE, LANES - MLA_NOPE - MLA_ROPE)))

    dscale = DIFF_HEAD_DIM ** -0.5 * LOG2_E
    win = jnp.concatenate([c_q, c_kv, slot(kr), slot(_swap_halves(kr)), dq * dscale, dk, dv, gates],
                          axis=1).astype(BF16)

    qscale = (MLA_NOPE + MLA_ROPE) ** -0.5 * LOG2_E
    uq = w_uq.reshape(MLA_Q_RANK, MLA_HEADS, MLA_NOPE + MLA_ROPE) * qscale
    pad_r = LANES - MLA_NOPE - MLA_ROPE
    wuqa = jnp.pad(uq, ((0, 0), (0, 0), (0, pad_r)))
    wuqb = jnp.pad(_swap_halves(uq[..., MLA_NOPE:]), ((0, 0), (0, 0), (MLA_NOPE, pad_r)))
    ukv = w_ukv.reshape(MLA_KV_RANK, MLA_HEADS, MLA_NOPE + MLA_V)
    wuk = jnp.pad(ukv[..., :MLA_NOPE], ((0, 0), (0, 0), (0, LANES - MLA_NOPE)))
    wuv = jnp.pad(ukv[..., MLA_NOPE:], ((0, 0), (0, 0), (0, LANES - MLA_V)))
    flat = lambda w: w.reshape(w.shape[0], MLA_HEADS * LANES).astype(BF16)
    return win, flat(wuqa), flat(wuqb), flat(wuk), flat(wuv)


def _rotary_rows():
    def inv_freq(rot_dim):
        half = rot_dim // 2
        return np.power(np.float32(ROPE_THETA), -2.0 * np.arange(half, dtype=np.float32) / rot_dim)

    rows = np.zeros((N_ROT_ROWS, LANES), np.float32)
    hm = MLA_ROPE // 2
    m0 = MLA_NOPE
    rows[R_FREQ, m0:m0 + MLA_ROPE] = np.tile(inv_freq(MLA_ROPE), 2)
    rows[R_MLA, m0:m0 + MLA_ROPE] = 1.0
    rows[R_MLA_SIGN, m0:m0 + hm] = -1.0
    rows[R_MLA_SIGN, m0 + hm:m0 + MLA_ROPE] = 1.0
    hd = DIFF_ROT // 2
    rows[R_FREQ, :DIFF_ROT] = np.tile(inv_freq(DIFF_ROT), 2)
    rows[R_ROT_LO, :DIFF_ROT] = 1.0
    rows[R_ROT_HI, DIFF_HEAD_DIM:DIFF_HEAD_DIM + DIFF_ROT] = 1.0
    for base in (0, DIFF_HEAD_DIM):
        rows[R_UP, base:base + hd] = -1.0
        rows[R_DN, base + hd:base + DIFF_ROT] = 1.0
    rows[R_LO, :DIFF_HEAD_DIM] = 1.0
    rows[R_HI, DIFF_HEAD_DIM:] = 1.0
    return jnp.asarray(rows)


def kernel(x, positions, ffn1_norm, ffn1_w_gate, ffn1_w_up, ffn1_w_down, mix_norm, w_in, mla_q_norm, mla_w_uq, mla_kv_norm, mla_w_ukv, diff_lam_q1, diff_lam_k1, diff_lam_q2, diff_lam_k2, diff_subln, w_proj_mla, w_proj_diff, w_out, ffn2_norm, ffn2_w_gate, ffn2_w_up, ffn2_w_down, final_norm):
    depth = ffn1_norm.shape[0]
    b, s, d = x.shape
    n = b * s
    x2d = x.reshape(n, d)
    posb = jnp.broadcast_to(positions.astype(F32).reshape(n, 1), (n, LANES))
    rot = _rotary_rows()
    final_w = final_norm.reshape(1, d)

    for l in range(depth):
        lam_init = 0.8 - 0.6 * math.exp(-0.3 * l)
        f1 = _ffn_weights(ffn1_w_gate[l], ffn1_w_up[l], ffn1_w_down[l])
        f2 = _ffn_weights(ffn2_w_gate[l], ffn2_w_up[l], ffn2_w_down[l])
        win, wuqa, wuqb, wuk, wuv = _inproj_weights(w_in[l], mla_w_uq[l], mla_w_ukv[l])
        pm = w_proj_mla[l].astype(BF16)
        pd = w_proj_diff[l].astype(BF16)
        wo = w_out[l].astype(BF16)
        lam_rows = jnp.stack([diff_lam_q1[l], diff_lam_k1[l], diff_lam_q2[l], diff_lam_k2[l]])

        x1 = _ffn(x2d, ffn1_norm[l].reshape(1, d), *f1, final_w, final=False, tm=512)
        qm, km, vm, qd1, qd2, kd, vd, g = _inproj(
            x1, posb, mix_norm[l].reshape(1, d), win, mla_q_norm[l].reshape(1, -1), wuqa, wuqb,
            mla_kv_norm[l].reshape(1, -1), wuk, wuv, rot, tm=512)

        def r3(t):
            return t.reshape(b, s, t.shape[1])

        ym = _flash([r3(qm)], r3(km), r3(vm), [], groups=MLA_HEADS // 2, heads=2, splits=2, rows=512,
                    l_row=MLA_V, finish=functools.partial(_mla_finish, splits=2), name="mla_attn")
        yd = _flash([r3(qd1), r3(qd2)], r3(kd), r3(vd), [lam_rows, diff_subln[l].reshape(1, -1)],
                    groups=DIFF_HEADS, heads=1, splits=2, rows=512, l_row=None,
                    finish=functools.partial(_diff_finish, lam_init=lam_init), name="diff_attn")
        x2 = _merge(x1, ym.reshape(n, -1), yd.reshape(n, -1), g, pm, pd, wo, tm=512)
        last = l == depth - 1
        x2d = _ffn(x2, ffn2_norm[l].reshape(1, d), *f2, final_w, final=last, tm=512)
    return x2d.reshape(b, s, d)
```

```python
import functools
import math

import numpy as np
import jax
import jax.numpy as jnp
from jax import lax
from jax.experimental import pallas as pl
from jax.experimental.pallas import tpu as pltpu

F32 = jnp.float32
BF16 = jnp.bfloat16

D_MODEL = 1024
D_FF = 2816
ROPE_THETA = 500000.0
EPS = 1e-6
LOG2_E = math.log2(math.e)

MLA_HEADS = 8
MLA_Q_RANK = 256
MLA_KV_RANK = 128
MLA_NOPE = 64
MLA_ROPE = 32
MLA_V = 64

DIFF_HEADS = 4
DIFF_HEAD_DIM = 64
DIFF_ROT = DIFF_HEAD_DIM // 4
DIFF_WIDTH = DIFF_HEADS * 2 * DIFF_HEAD_DIM

LANES = 128
BF16_ROWS = 16
FF_CHUNK = 256
N_FF_CHUNKS = D_FF // FF_CHUNK
VMEM_LIMIT = 56 * 1024 * 1024

C_CQ = 0
C_CKV = C_CQ + MLA_Q_RANK
C_KRA = C_CKV + MLA_KV_RANK
C_KRB = C_KRA + LANES
C_DQ = C_KRB + LANES
C_DK = C_DQ + DIFF_WIDTH
C_DV = C_DK + DIFF_WIDTH
C_GATE = C_DV + DIFF_WIDTH
N_IN_COLS = C_GATE + 2 * D_MODEL

(R_FREQ, R_MLA, R_MLA_SIGN, R_ROT_LO, R_ROT_HI, R_UP, R_DN, R_LO, R_HI) = range(9)
N_ROT_ROWS = 16


def _rms(xf, w):
    ms = jnp.mean(xf * xf, axis=-1, keepdims=True)
    return xf * lax.rsqrt(ms + EPS) * w


def _dot(a, b):
    return jnp.dot(a, b, preferred_element_type=F32)


def _resident(shape):
    nd = len(shape)
    return pl.BlockSpec(shape, lambda *_: (0,) * nd, pipeline_mode=pl.Buffered(1))


def _ffn_kernel(x_ref, nw_ref, wg_ref, wu_ref, wd_ref, fw_ref, o_ref, *, final):
    x = x_ref[...]
    xn = _rms(x, nw_ref[...]).astype(BF16)
    acc = None
    for c in range(N_FF_CHUNKS):
        ff = slice(c * FF_CHUNK, (c + 1) * FF_CHUNK)
        g = _dot(xn, wg_ref[:, ff])
        u = _dot(xn, wu_ref[:, ff])
        a = (g * jax.nn.sigmoid(g) * u).astype(BF16)
        d = _dot(a, wd_ref[ff, :])
        acc = d if acc is None else acc + d
    y = x + 0.5 * acc
    if final:
        y = _rms(y, fw_ref[...])
    o_ref[...] = y


def _ffn(x2d, norm_w, wg, wu, wd, final_w, *, final, tm):
    n = x2d.shape[0]
    row = pl.BlockSpec((tm, D_MODEL), lambda i: (i, 0))
    return pl.pallas_call(
        functools.partial(_ffn_kernel, final=final),
        grid=(n // tm,),
        in_specs=[row, _resident((1, D_MODEL)), _resident(wg.shape), _resident(wu.shape),
                  _resident(wd.shape), _resident((1, D_MODEL))],
        out_specs=row,
        out_shape=jax.ShapeDtypeStruct((n, D_MODEL), F32),
        compiler_params=pltpu.CompilerParams(dimension_semantics=("arbitrary",),
                                             vmem_limit_bytes=VMEM_LIMIT),
        name="ffn_final" if final else "ffn",
    )(x2d, norm_w, wg, wu, wd, final_w)


def _inproj_kernel(x_ref, pos_ref, mixw_ref, win_ref, qnw_ref, wuqa_ref, wuqb_ref, kvnw_ref,
                   wuk_ref, wuv_ref, rot_ref,
                   qm_ref, km_ref, vm_ref, qd1_ref, qd2_ref, kd_ref, vd_ref, g_ref):
    h = _rms(x_ref[...], mixw_ref[...]).astype(BF16)
    rot = rot_ref[...]
    ang = pos_ref[...] * rot[R_FREQ:R_FREQ + 1]
    cos_a, sin_a = jnp.cos(ang), jnp.sin(ang)
    cos_m = jnp.where(rot[R_MLA:R_MLA + 1] > 0, cos_a, 1.0)
    sin_m = sin_a * rot[R_MLA_SIGN:R_MLA_SIGN + 1]
    cos_hi = pltpu.roll(cos_a, DIFF_HEAD_DIM, axis=1)
    sin_hi = pltpu.roll(sin_a, DIFF_HEAD_DIM, axis=1)
    lo, hi = rot[R_LO:R_LO + 1], rot[R_HI:R_HI + 1]
    rot_lo, rot_hi = rot[R_ROT_LO:R_ROT_LO + 1], rot[R_ROT_HI:R_ROT_HI + 1]
    cos_d = jnp.where(rot_lo > 0, cos_a, jnp.where(rot_hi > 0, cos_hi, 1.0))
    sin_d = sin_a * rot_lo + sin_hi * rot_hi
    sin_up = sin_d * rot[R_UP:R_UP + 1]
    sin_dn = sin_d * rot[R_DN:R_DN + 1]

    z0 = _dot(h, win_ref[:, C_CQ:C_DQ])
    cqn = _rms(z0[:, C_CQ:C_CKV], qnw_ref[...]).astype(BF16)
    kvn = _rms(z0[:, C_CKV:C_KRA], kvnw_ref[...]).astype(BF16)
    k_rope = z0[:, C_KRA:C_KRB] * cos_m + z0[:, C_KRB:C_DQ] * sin_m

    qa = _dot(cqn, wuqa_ref[...])
    qb = _dot(cqn, wuqb_ref[...])
    ka = _dot(kvn, wuk_ref[...])
    for hd in range(MLA_HEADS):
        sl = slice(hd * LANES, (hd + 1) * LANES)
        qm_ref[:, sl] = (qa[:, sl] * cos_m + qb[:, sl] * sin_m).astype(BF16)
        km_ref[:, sl] = (ka[:, sl] + k_rope).astype(BF16)
    vm = _dot(kvn, wuv_ref[...])
    lane = lax.broadcasted_iota(jnp.int32, vm.shape, 1) % LANES
    vm_ref[...] = jnp.where(lane == MLA_V, 1.0, vm).astype(BF16)

    half = DIFF_ROT // 2

    head_lanes = [slice(hd * LANES, (hd + 1) * LANES) for hd in range(DIFF_HEADS)]

    def diff_rope(x):
        up = pltpu.roll(x, DIFF_WIDTH - half, axis=1)
        dn = pltpu.roll(x, half, axis=1)
        return [x[:, sl] * cos_d + up[:, sl] * sin_up + dn[:, sl] * sin_dn for sl in head_lanes]

    dq = diff_rope(_dot(h, win_ref[:, C_DQ:C_DK]))
    dk = diff_rope(_dot(h, win_ref[:, C_DK:C_DV]))
    for sl, qr, kr in zip(head_lanes, dq, dk):
        qd1_ref[:, sl] = (qr * lo).astype(BF16)
        qd2_ref[:, sl] = (qr * hi).astype(BF16)
        kd_ref[:, sl] = kr.astype(BF16)
    vd_ref[...] = _dot(h, win_ref[:, C_DV:C_GATE]).astype(BF16)
    g_ref[...] = jax.nn.sigmoid(_dot(h, win_ref[:, C_GATE:N_IN_COLS])).astype(BF16)


def _inproj(x1, posb, mixw, win, qnw, wuqa, wuqb, kvnw, wuk, wuv, rot, *, tm):
    n = x1.shape[0]

    def row(w):
        return pl.BlockSpec((tm, w), lambda i: (i, 0))

    wide = MLA_HEADS * LANES
    out_widths = (wide, wide, wide, DIFF_WIDTH, DIFF_WIDTH, DIFF_WIDTH, DIFF_WIDTH, 2 * D_MODEL)
    return pl.pallas_call(
        _inproj_kernel,
        grid=(n // tm,),
        in_specs=[row(D_MODEL), row(LANES), _resident(mixw.shape), _resident(win.shape),
                  _resident(qnw.shape), _resident(wuqa.shape), _resident(wuqb.shape),
                  _resident(kvnw.shape), _resident(wuk.shape), _resident(wuv.shape),
                  _resident(rot.shape)],
        out_specs=[row(w) for w in out_widths],
        out_shape=[jax.ShapeDtypeStruct((n, w), BF16) for w in out_widths],
        compiler_params=pltpu.CompilerParams(dimension_semantics=("arbitrary",),
                                             vmem_limit_bytes=VMEM_LIMIT),
        name="inproj",
    )(x1, posb, mixw, win, qnw, wuqa, wuqb, kvnw, wuk, wuv, rot)


class _Chain:
    def __init__(self, q, head, off, refs):
        self.q, self.head, self.off = q, head, off
        self.m, self.acc = refs[0:2]
        self.s = refs[2:4]
        self.smax = refs[4:6]


_CHAIN_REFS = 6


def _score_tile(ch, slot, k, masked):
    s = lax.dot_general(k, ch.q, (((1,), (1,)), ((), ())), preferred_element_type=F32)
    if masked:
        kidx = lax.broadcasted_iota(jnp.int32, s.shape, 0)
        qidx = lax.broadcasted_iota(jnp.int32, s.shape, 1)
        s = jnp.where(kidx <= qidx, s, -jnp.inf)
    ch.s[slot][...] = s
    ch.smax[slot][...] = jnp.max(s, axis=0, keepdims=True)


def _softmax_update(ch, slot, vt):
    m_prev = ch.m[...]
    m_new = jnp.maximum(m_prev, ch.smax[slot][...])
    alpha = jnp.exp2(m_prev - m_new)
    p = jnp.exp2(ch.s[slot][...] - m_new)
    ch.acc[...] = alpha * ch.acc[...] + _dot(vt, p.astype(BF16))
    ch.m[...] = m_new


def _flash_kernel(*refs, n_q, heads, splits, rows, l_row, acc_rows, finish):
    assert splits % 2 == 0
    n_chains = n_q * heads * splits
    q_refs = refs[:n_q]
    k_ref, v_ref = refs[n_q], refs[n_q + 1]
    n_scratch = _CHAIN_REFS * n_chains
    extra = refs[n_q + 2:len(refs) - 1 - n_scratch]
    o_ref = refs[len(refs) - 1 - n_scratch]
    scratch = iter(refs[len(refs) - n_scratch:])
    chains = [_Chain(q_refs[i][0, c * rows:(c + 1) * rows, h * LANES:(h + 1) * LANES], h, c,
                     [next(scratch) for _ in range(_CHAIN_REFS)])
              for i in range(n_q) for h in range(heads) for c in range(splits)]
    for ch in chains:
        ch.m[...] = jnp.full_like(ch.m, -jnp.inf)
        ch.acc[...] = jnp.zeros_like(ch.acc)

    base = pl.program_id(2) * splits

    def tile(ref, j, head):
        return ref[0, pl.ds(pl.multiple_of(j * rows, rows), rows), head * LANES:(head + 1) * LANES]

    def value_rows(j, head):
        vt = tile(v_ref, j, head).T
        if l_row < LANES:
            return vt[:acc_rows]
        return jnp.concatenate([vt, jnp.ones((acc_rows - LANES, rows), BF16)], axis=0)

    def scores(ch, j, slot, diag=None):
        if diag is None or ch.off >= diag:
            _score_tile(ch, slot, tile(k_ref, j, ch.head), masked=(ch.off == diag))

    def update(ch, j, slot, diag=None):
        if diag is None or ch.off >= diag:
            _softmax_update(ch, slot, value_rows(j, ch.head))

    def step(j, slot, diag=None, next_diag=None, last=False):
        for ch in chains:
            if not last:
                scores(ch, j + 1, 1 - slot, next_diag)
            update(ch, j, slot, diag)

    def two_tiles(jj, carry):
        step(2 * jj, 0)
        step(2 * jj + 1, 1)
        return carry

    @pl.when(base > 0)
    def _():
        for ch in chains:
            scores(ch, 0, 0)
        lax.fori_loop(0, base // 2 - 1, two_tiles, 0)
        step(base - 2, 0)
        step(base - 1, 1, next_diag=0)

    @pl.when(base == 0)
    def _():
        for ch in chains:
            scores(ch, 0, 0, diag=0)

    for d in range(splits):
        step(base + d, d % 2, diag=d, next_diag=d + 1, last=(d + 1 == splits))

    outs = []
    for ch in chains:
        acc = ch.acc[...]
        outs.append(acc[:min(LANES, acc_rows)] / acc[l_row:l_row + 1])
    finish(outs, o_ref, *extra)


def _flash(qs, k, v, extra, *, groups, heads, splits, rows, l_row, acc_rows, finish, name):
    b, s, _ = k.shape
    n_q = len(qs)
    tq = splits * rows
    q_spec = pl.BlockSpec((1, tq, heads * LANES), lambda bi, gi, qi: (bi, qi, gi))
    kv_spec = pl.BlockSpec((1, s, heads * LANES), lambda bi, gi, qi: (bi, 0, gi))
    chain_scratch = ([pltpu.VMEM((1, rows), F32), pltpu.VMEM((acc_rows, rows), F32)]
                     + [pltpu.VMEM((rows, rows), F32)] * 2 + [pltpu.VMEM((1, rows), F32)] * 2)
    return pl.pallas_call(
        functools.partial(_flash_kernel, n_q=n_q, heads=heads, splits=splits, rows=rows, l_row=l_row,
                          acc_rows=acc_rows, finish=finish),
        grid=(b, groups, s // tq),
        in_specs=[q_spec] * n_q + [kv_spec, kv_spec] + [_resident(e.shape) for e in extra],
        out_specs=pl.BlockSpec((1, tq, LANES), lambda bi, gi, qi: (bi, qi, gi)),
        out_shape=jax.ShapeDtypeStruct((b, s, groups * LANES), BF16),
        scratch_shapes=chain_scratch * (n_q * heads * splits),
        compiler_params=pltpu.CompilerParams(
            dimension_semantics=("arbitrary", "arbitrary", "arbitrary"),
            vmem_limit_bytes=VMEM_LIMIT),
        name=name,
    )(*qs, k, v, *extra)


def _mla_finish(outs, o_ref, *, splits):
    rows = outs[0].shape[1]
    for c in range(splits):
        pair = jnp.concatenate([outs[c][:MLA_V], outs[splits + c][:MLA_V]], axis=0)
        o_ref[0, c * rows:(c + 1) * rows, :] = pair.T.astype(o_ref.dtype)


def _diff_finish(outs, o_ref, lam_ref, subln_ref, *, lam_init):
    lam_v = lam_ref[...]
    e1 = jnp.exp(jnp.sum(lam_v[0:1] * lam_v[1:2], axis=1, keepdims=True))
    e2 = jnp.exp(jnp.sum(lam_v[2:3] * lam_v[3:4], axis=1, keepdims=True))
    lam = e1 - e2 + lam_init
    splits = len(outs) // 2
    rows = outs[0].shape[1]
    for c in range(splits):
        o = (outs[c] - lam * outs[splits + c]).T
        o_ref[0, c * rows:(c + 1) * rows, :] = (_rms(o, subln_ref[...]) * (1.0 - lam_init)).astype(o_ref.dtype)


def _merge_kernel(x_ref, ym_ref, yd_ref, g_ref, pm_ref, pd_ref, wo_ref, o_ref):
    g = g_ref[...]
    m = (g[:, :D_MODEL] * _dot(ym_ref[...], pm_ref[...])
         + g[:, D_MODEL:] * _dot(yd_ref[...], pd_ref[...]))
    o_ref[...] = x_ref[...] + _dot(m.astype(BF16), wo_ref[...])


def _merge(x1, ym, yd, g, pm, pd, wo, *, tm):
    n = x1.shape[0]

    def row(w):
        return pl.BlockSpec((tm, w), lambda i: (i, 0))

    return pl.pallas_call(
        _merge_kernel,
        grid=(n // tm,),
        in_specs=[row(D_MODEL), row(ym.shape[1]), row(yd.shape[1]), row(g.shape[1]),
                  _resident(pm.shape), _resident(pd.shape), _resident(wo.shape)],
        out_specs=row(D_MODEL),
        out_shape=jax.ShapeDtypeStruct((n, D_MODEL), F32),
        compiler_params=pltpu.CompilerParams(dimension_semantics=("arbitrary",),
                                             vmem_limit_bytes=VMEM_LIMIT),
        name="merge",
    )(x1, ym, yd, g, pm, pd, wo)


def _ffn_weights(wg, wu, wd):
    return wg.astype(BF16), wu.astype(BF16), wd.astype(BF16)


def _swap_halves(w):
    half = w.shape[-1] // 2
    return jnp.concatenate([w[..., half:], w[..., :half]], axis=-1)


def _inproj_weights(w_in, w_uq, w_ukv):
    d = w_in.shape[0]
    cuts = np.cumsum((MLA_Q_RANK, MLA_KV_RANK, MLA_ROPE, DIFF_WIDTH, DIFF_WIDTH, DIFF_WIDTH, D_MODEL))
    c_q, c_kv, kr, dq, dk, dv, gates = (w_in[:, a:b] for a, b in
                                        zip((0, *cuts[:-1]), cuts))
    gates = jnp.concatenate([gates, w_in[:, cuts[-1]:]], axis=1)

    def slot(w):
        return jnp.pad(w, ((0, 0), (MLA_NOPE, LANES - MLA_NOPE - MLA_ROPE)))

    dscale = DIFF_HEAD_DIM ** -0.5 * LOG2_E
    win = jnp.concatenate([c_q, c_kv, slot(kr), slot(_swap_halves(kr)), dq * dscale, dk, dv, gates],
                          axis=1).astype(BF16)

    qscale = (MLA_NOPE + MLA_ROPE) ** -0.5 * LOG2_E
    uq = w_uq.reshape(MLA_Q_RANK, MLA_HEADS, MLA_NOPE + MLA_ROPE) * qscale
    pad_r = LANES - MLA_NOPE - MLA_ROPE
    wuqa = jnp.pad(uq, ((0, 0), (0, 0), (0, pad_r)))
    wuqb = jnp.pad(_swap_halves(uq[..., MLA_NOPE:]), ((0, 0), (0, 0), (MLA_NOPE, pad_r)))
    ukv = w_ukv.reshape(MLA_KV_RANK, MLA_HEADS, MLA_NOPE + MLA_V)
    wuk = jnp.pad(ukv[..., :MLA_NOPE], ((0, 0), (0, 0), (0, LANES - MLA_NOPE)))
    wuv = jnp.pad(ukv[..., MLA_NOPE:], ((0, 0), (0, 0), (0, LANES - MLA_V)))
    flat = lambda w: w.reshape(w.shape[0], MLA_HEADS * LANES).astype(BF16)
    return win, flat(wuqa), flat(wuqb), flat(wuk), flat(wuv)


def _rotary_rows():
    def inv_freq(rot_dim):
        half = rot_dim // 2
        return np.power(np.float32(ROPE_THETA), -2.0 * np.arange(half, dtype=np.float32) / rot_dim)

    rows = np.zeros((N_ROT_ROWS, LANES), np.float32)
    hm = MLA_ROPE // 2
    m0 = MLA_NOPE
    rows[R_FREQ, m0:m0 + MLA_ROPE] = np.tile(inv_freq(MLA_ROPE), 2)
    rows[R_MLA, m0:m0 + MLA_ROPE] = 1.0
    rows[R_MLA_SIGN, m0:m0 + hm] = -1.0
    rows[R_MLA_SIGN, m0 + hm:m0 + MLA_ROPE] = 1.0
    hd = DIFF_ROT // 2
    rows[R_FREQ, :DIFF_ROT] = np.tile(inv_freq(DIFF_ROT), 2)
    rows[R_ROT_LO, :DIFF_ROT] = 1.0
    rows[R_ROT_HI, DIFF_HEAD_DIM:DIFF_HEAD_DIM + DIFF_ROT] = 1.0
    for base in (0, DIFF_HEAD_DIM):
        rows[R_UP, base:base + hd] = -1.0
        rows[R_DN, base + hd:base + DIFF_ROT] = 1.0
    rows[R_LO, :DIFF_HEAD_DIM] = 1.0
    rows[R_HI, DIFF_HEAD_DIM:] = 1.0
    return jnp.asarray(rows)


def kernel(x, positions, ffn1_norm, ffn1_w_gate, ffn1_w_up, ffn1_w_down, mix_norm, w_in, mla_q_norm, mla_w_uq, mla_kv_norm, mla_w_ukv, diff_lam_q1, diff_lam_k1, diff_lam_q2, diff_lam_k2, diff_subln, w_proj_mla, w_proj_diff, w_out, ffn2_norm, ffn2_w_gate, ffn2_w_up, ffn2_w_down, final_norm):
    depth = ffn1_norm.shape[0]
    b, s, d = x.shape
    n = b * s
    x2d = x.reshape(n, d)
    posb = jnp.broadcast_to(positions.astype(F32).reshape(n, 1), (n, LANES))
    rot = _rotary_rows()
    final_w = final_norm.reshape(1, d)

    for l in range(depth):
        lam_init = 0.8 - 0.6 * math.exp(-0.3 * l)
        f1 = _ffn_weights(ffn1_w_gate[l], ffn1_w_up[l], ffn1_w_down[l])
        f2 = _ffn_weights(ffn2_w_gate[l], ffn2_w_up[l], ffn2_w_down[l])
        win, wuqa, wuqb, wuk, wuv = _inproj_weights(w_in[l], mla_w_uq[l], mla_w_ukv[l])
        pm = w_proj_mla[l].astype(BF16)
        pd = w_proj_diff[l].astype(BF16)
        wo = w_out[l].astype(BF16)
        lam_rows = jnp.stack([diff_lam_q1[l], diff_lam_k1[l], diff_lam_q2[l], diff_lam_k2[l]])

        x1 = _ffn(x2d, ffn1_norm[l].reshape(1, d), *f1, final_w, final=False, tm=512)
        qm, km, vm, qd1, qd2, kd, vd, g = _inproj(
            x1, posb, mix_norm[l].reshape(1, d), win, mla_q_norm[l].reshape(1, -1), wuqa, wuqb,
            mla_kv_norm[l].reshape(1, -1), wuk, wuv, rot, tm=512)

        def r3(t):
            return t.reshape(b, s, t.shape[1])

        ym = _flash([r3(qm)], r3(km), r3(vm), [], groups=MLA_HEADS // 2, heads=2, splits=2, rows=512,
                    l_row=MLA_V, acc_rows=MLA_V + BF16_ROWS, finish=functools.partial(_mla_finish, splits=2),
                    name="mla_attn")
        yd = _flash([r3(qd1), r3(qd2)], r3(kd), r3(vd), [lam_rows, diff_subln[l].reshape(1, -1)],
                    groups=DIFF_HEADS, heads=1, splits=2, rows=512, l_row=LANES, acc_rows=LANES + BF16_ROWS,
                    finish=functools.partial(_diff_finish, lam_init=lam_init), name="diff_attn")
        x2 = _merge(x1, ym.reshape(n, -1), yd.reshape(n, -1), g, pm, pd, wo, tm=512)
        last = l == depth - 1
        x2d = _ffn(x2, ffn2_norm[l].reshape(1, d), *f2, final_w, final=last, tm=512)
    return x2d.reshape(b, s, d)
```

```python
import functools
import math

import numpy as np
import jax
import jax.numpy as jnp
from jax import lax
from jax.experimental import pallas as pl
from jax.experimental.pallas import tpu as pltpu

F32 = jnp.float32
BF16 = jnp.bfloat16

D_MODEL = 1024
D_FF = 2816
ROPE_THETA = 500000.0
EPS = 1e-6
LOG2_E = math.log2(math.e)

MLA_HEADS = 8
MLA_Q_RANK = 256
MLA_KV_RANK = 128
MLA_NOPE = 64
MLA_ROPE = 32
MLA_V = 64

DIFF_HEADS = 4
DIFF_HEAD_DIM = 64
DIFF_ROT = DIFF_HEAD_DIM // 4
DIFF_WIDTH = DIFF_HEADS * 2 * DIFF_HEAD_DIM

LANES = 128
BF16_ROWS = 16
FF_CHUNK = 256
N_FF_CHUNKS = D_FF // FF_CHUNK
VMEM_LIMIT = 56 * 1024 * 1024

C_CQ = 0
C_CKV = C_CQ + MLA_Q_RANK
C_KRA = C_CKV + MLA_KV_RANK
C_KRB = C_KRA + LANES
C_DQ = C_KRB + LANES
C_DK = C_DQ + DIFF_WIDTH
C_DV = C_DK + DIFF_WIDTH
C_GATE = C_DV + DIFF_WIDTH
N_IN_COLS = C_GATE + 2 * D_MODEL

(R_FREQ, R_MLA, R_MLA_SIGN, R_ROT_LO, R_ROT_HI, R_UP, R_DN, R_LO, R_HI) = range(9)
N_ROT_ROWS = 16


def _rms(xf, w):
    ms = jnp.mean(xf * xf, axis=-1, keepdims=True)
    return xf * lax.rsqrt(ms + EPS) * w


def _dot(a, b):
    return jnp.dot(a, b, preferred_element_type=F32)


def _resident(shape):
    nd = len(shape)
    return pl.BlockSpec(shape, lambda *_: (0,) * nd, pipeline_mode=pl.Buffered(1))


def _swiglu_half_step(x, nw_ref, wg_ref, wu_ref, wd_ref):
    xn = _rms(x, nw_ref[...]).astype(BF16)
    acc = None
    for c in range(N_FF_CHUNKS):
        ff = slice(c * FF_CHUNK, (c + 1) * FF_CHUNK)
        g = _dot(xn, wg_ref[:, ff])
        u = _dot(xn, wu_ref[:, ff])
        a = (g * jax.nn.sigmoid(g) * u).astype(BF16)
        d = _dot(a, wd_ref[ff, :])
        acc = d if acc is None else acc + d
    return x + 0.5 * acc


def _ffn_kernel(x_ref, nw_ref, wg_ref, wu_ref, wd_ref, o_ref):
    o_ref[...] = _swiglu_half_step(x_ref[...], nw_ref, wg_ref, wu_ref, wd_ref)


def _ffn(x2d, norm_w, wg, wu, wd, *, tm):
    n = x2d.shape[0]
    row = pl.BlockSpec((tm, D_MODEL), lambda i: (i, 0))
    return pl.pallas_call(
        _ffn_kernel,
        grid=(n // tm,),
        in_specs=[row, _resident((1, D_MODEL)), _resident(wg.shape), _resident(wu.shape), _resident(wd.shape)],
        out_specs=row,
        out_shape=jax.ShapeDtypeStruct((n, D_MODEL), F32),
        compiler_params=pltpu.CompilerParams(dimension_semantics=("arbitrary",),
                                             vmem_limit_bytes=VMEM_LIMIT),
        name="ffn",
    )(x2d, norm_w, wg, wu, wd)


def _inproj_kernel(x_ref, pos_ref, mixw_ref, win_ref, qnw_ref, wuqa_ref, wuqb_ref, kvnw_ref,
                   wuk_ref, wuv_ref, rot_ref,
                   qm_ref, km_ref, vm_ref, qd1_ref, qd2_ref, kd_ref, vd_ref, g_ref):
    h = _rms(x_ref[...], mixw_ref[...]).astype(BF16)
    rot = rot_ref[...]
    ang = pos_ref[...] * rot[R_FREQ:R_FREQ + 1]
    cos_a, sin_a = jnp.cos(ang), jnp.sin(ang)
    cos_m = jnp.where(rot[R_MLA:R_MLA + 1] > 0, cos_a, 1.0)
    sin_m = sin_a * rot[R_MLA_SIGN:R_MLA_SIGN + 1]
    cos_hi = pltpu.roll(cos_a, DIFF_HEAD_DIM, axis=1)
    sin_hi = pltpu.roll(sin_a, DIFF_HEAD_DIM, axis=1)
    lo, hi = rot[R_LO:R_LO + 1], rot[R_HI:R_HI + 1]
    rot_lo, rot_hi = rot[R_ROT_LO:R_ROT_LO + 1], rot[R_ROT_HI:R_ROT_HI + 1]
    cos_d = jnp.where(rot_lo > 0, cos_a, jnp.where(rot_hi > 0, cos_hi, 1.0))
    sin_d = sin_a * rot_lo + sin_hi * rot_hi
    sin_up = sin_d * rot[R_UP:R_UP + 1]
    sin_dn = sin_d * rot[R_DN:R_DN + 1]

    z0 = _dot(h, win_ref[:, C_CQ:C_DQ])
    cqn = _rms(z0[:, C_CQ:C_CKV], qnw_ref[...]).astype(BF16)
    kvn = _rms(z0[:, C_CKV:C_KRA], kvnw_ref[...]).astype(BF16)
    k_rope = z0[:, C_KRA:C_KRB] * cos_m + z0[:, C_KRB:C_DQ] * sin_m

    qa = _dot(cqn, wuqa_ref[...])
    qb = _dot(cqn, wuqb_ref[...])
    ka = _dot(kvn, wuk_ref[...])
    for hd in range(MLA_HEADS):
        sl = slice(hd * LANES, (hd + 1) * LANES)
        qm_ref[:, sl] = (qa[:, sl] * cos_m + qb[:, sl] * sin_m).astype(BF16)
        km_ref[:, sl] = (ka[:, sl] + k_rope).astype(BF16)
    vm = _dot(kvn, wuv_ref[...])
    lane = lax.broadcasted_iota(jnp.int32, vm.shape, 1) % LANES
    vm_ref[...] = jnp.where(lane == MLA_V, 1.0, vm).astype(BF16)

    half = DIFF_ROT // 2

    head_lanes = [slice(hd * LANES, (hd + 1) * LANES) for hd in range(DIFF_HEADS)]

    def diff_rope(x):
        up = pltpu.roll(x, DIFF_WIDTH - half, axis=1)
        dn = pltpu.roll(x, half, axis=1)
        return [x[:, sl] * cos_d + up[:, sl] * sin_up + dn[:, sl] * sin_dn for sl in head_lanes]

    dq = diff_rope(_dot(h, win_ref[:, C_DQ:C_DK]))
    dk = diff_rope(_dot(h, win_ref[:, C_DK:C_DV]))
    for sl, qr, kr in zip(head_lanes, dq, dk):
        qd1_ref[:, sl] = (qr * lo).astype(BF16)
        qd2_ref[:, sl] = (qr * hi).astype(BF16)
        kd_ref[:, sl] = kr.astype(BF16)
    vd_ref[...] = _dot(h, win_ref[:, C_DV:C_GATE]).astype(BF16)
    g_ref[...] = jax.nn.sigmoid(_dot(h, win_ref[:, C_GATE:N_IN_COLS])).astype(BF16)


def _inproj(x1, posb, mixw, win, qnw, wuqa, wuqb, kvnw, wuk, wuv, rot, *, tm):
    n = x1.shape[0]

    def row(w):
        return pl.BlockSpec((tm, w), lambda i: (i, 0))

    wide = MLA_HEADS * LANES
    out_widths = (wide, wide, wide, DIFF_WIDTH, DIFF_WIDTH, DIFF_WIDTH, DIFF_WIDTH, 2 * D_MODEL)
    return pl.pallas_call(
        _inproj_kernel,
        grid=(n // tm,),
        in_specs=[row(D_MODEL), row(LANES), _resident(mixw.shape), _resident(win.shape),
                  _resident(qnw.shape), _resident(wuqa.shape), _resident(wuqb.shape),
                  _resident(kvnw.shape), _resident(wuk.shape), _resident(wuv.shape),
                  _resident(rot.shape)],
        out_specs=[row(w) for w in out_widths],
        out_shape=[jax.ShapeDtypeStruct((n, w), BF16) for w in out_widths],
        compiler_params=pltpu.CompilerParams(dimension_semantics=("arbitrary",),
                                             vmem_limit_bytes=VMEM_LIMIT),
        name="inproj",
    )(x1, posb, mixw, win, qnw, wuqa, wuqb, kvnw, wuk, wuv, rot)


class _Chain:
    def __init__(self, q, head, off, refs):
        self.q, self.head, self.off = q, head, off
        self.m, self.acc = refs[0:2]
        self.s = refs[2:4]
        self.smax = refs[4:6]


_CHAIN_REFS = 6


def _score_tile(ch, slot, k, masked):
    s = lax.dot_general(k, ch.q, (((1,), (1,)), ((), ())), preferred_element_type=F32)
    if masked:
        kidx = lax.broadcasted_iota(jnp.int32, s.shape, 0)
        qidx = lax.broadcasted_iota(jnp.int32, s.shape, 1)
        s = jnp.where(kidx <= qidx, s, -jnp.inf)
    ch.s[slot][...] = s
    ch.smax[slot][...] = jnp.max(s, axis=0, keepdims=True)


def _softmax_update(ch, slot, vt):
    m_prev = ch.m[...]
    m_new = jnp.maximum(m_prev, ch.smax[slot][...])
    alpha = jnp.exp2(m_prev - m_new)
    p = jnp.exp2(ch.s[slot][...] - m_new)
    ch.acc[...] = alpha * ch.acc[...] + _dot(vt, p.astype(BF16))
    ch.m[...] = m_new


def _flash_kernel(*refs, n_q, heads, splits, rows, l_row, acc_rows, finish):
    assert splits % 2 == 0
    n_chains = n_q * heads * splits
    q_refs = refs[:n_q]
    k_ref, v_ref = refs[n_q], refs[n_q + 1]
    n_scratch = _CHAIN_REFS * n_chains
    extra = refs[n_q + 2:len(refs) - 1 - n_scratch]
    o_ref = refs[len(refs) - 1 - n_scratch]
    scratch = iter(refs[len(refs) - n_scratch:])
    chains = [_Chain(q_refs[i][0, c * rows:(c + 1) * rows, h * LANES:(h + 1) * LANES], h, c,
                     [next(scratch) for _ in range(_CHAIN_REFS)])
              for i in range(n_q) for h in range(heads) for c in range(splits)]
    for ch in chains:
        ch.m[...] = jnp.full_like(ch.m, -jnp.inf)
        ch.acc[...] = jnp.zeros_like(ch.acc)

    base = pl.program_id(2) * splits

    def tile(ref, j, head):
        return ref[0, pl.ds(pl.multiple_of(j * rows, rows), rows), head * LANES:(head + 1) * LANES]

    def value_rows(j, head):
        vt = tile(v_ref, j, head).T
        if l_row < LANES:
            return vt[:acc_rows]
        return jnp.concatenate([vt, jnp.ones((acc_rows - LANES, rows), BF16)], axis=0)

    def scores(ch, j, slot, diag=None):
        if diag is None or ch.off >= diag:
            _score_tile(ch, slot, tile(k_ref, j, ch.head), masked=(ch.off == diag))

    def update(ch, j, slot, diag=None):
        if diag is None or ch.off >= diag:
            _softmax_update(ch, slot, value_rows(j, ch.head))

    def step(j, slot, diag=None, next_diag=None, last=False):
        for ch in chains:
            if not last:
                scores(ch, j + 1, 1 - slot, next_diag)
            update(ch, j, slot, diag)

    def two_tiles(jj, carry):
        step(2 * jj, 0)
        step(2 * jj + 1, 1)
        return carry

    @pl.when(base > 0)
    def _():
        for ch in chains:
            scores(ch, 0, 0)
        lax.fori_loop(0, base // 2 - 1, two_tiles, 0)
        step(base - 2, 0)
        step(base - 1, 1, next_diag=0)

    @pl.when(base == 0)
    def _():
        for ch in chains:
            scores(ch, 0, 0, diag=0)

    for d in range(splits):
        step(base + d, d % 2, diag=d, next_diag=d + 1, last=(d + 1 == splits))

    outs = []
    for ch in chains:
        acc = ch.acc[...]
        outs.append(acc[:min(LANES, acc_rows)] / acc[l_row:l_row + 1])
    finish(outs, o_ref, *extra)


def _flash(qs, k, v, extra, *, groups, heads, splits, rows, l_row, acc_rows, finish, name):
    b, s, _ = k.shape
    n_q = len(qs)
    tq = splits * rows
    q_spec = pl.BlockSpec((1, tq, heads * LANES), lambda bi, gi, qi: (bi, qi, gi))
    kv_spec = pl.BlockSpec((1, s, heads * LANES), lambda bi, gi, qi: (bi, 0, gi))
    chain_scratch = ([pltpu.VMEM((1, rows), F32), pltpu.VMEM((acc_rows, rows), F32)]
                     + [pltpu.VMEM((rows, rows), F32)] * 2 + [pltpu.VMEM((1, rows), F32)] * 2)
    return pl.pallas_call(
        functools.partial(_flash_kernel, n_q=n_q, heads=heads, splits=splits, rows=rows, l_row=l_row,
                          acc_rows=acc_rows, finish=finish),
        grid=(b, groups, s // tq),
        in_specs=[q_spec] * n_q + [kv_spec, kv_spec] + [_resident(e.shape) for e in extra],
        out_specs=pl.BlockSpec((1, tq, LANES), lambda bi, gi, qi: (bi, qi, gi)),
        out_shape=jax.ShapeDtypeStruct((b, s, groups * LANES), BF16),
        scratch_shapes=chain_scratch * (n_q * heads * splits),
        compiler_params=pltpu.CompilerParams(
            dimension_semantics=("arbitrary", "arbitrary", "arbitrary"),
            vmem_limit_bytes=VMEM_LIMIT),
        name=name,
    )(*qs, k, v, *extra)


def _mla_finish(outs, o_ref, *, splits):
    rows = outs[0].shape[1]
    for c in range(splits):
        pair = jnp.concatenate([outs[c][:MLA_V], outs[splits + c][:MLA_V]], axis=0)
        o_ref[0, c * rows:(c + 1) * rows, :] = pair.T.astype(o_ref.dtype)


def _diff_finish(outs, o_ref, lam_ref, subln_ref, *, lam_init):
    lam_v = lam_ref[...]
    e1 = jnp.exp(jnp.sum(lam_v[0:1] * lam_v[1:2], axis=1, keepdims=True))
    e2 = jnp.exp(jnp.sum(lam_v[2:3] * lam_v[3:4], axis=1, keepdims=True))
    lam = e1 - e2 + lam_init
    splits = len(outs) // 2
    rows = outs[0].shape[1]
    for c in range(splits):
        o = (outs[c] - lam * outs[splits + c]).T
        o_ref[0, c * rows:(c + 1) * rows, :] = (_rms(o, subln_ref[...]) * (1.0 - lam_init)).astype(o_ref.dtype)


def _merge_ffn_kernel(x_ref, ym_ref, yd_ref, g_ref, pm_ref, pd_ref, wo_ref,
                      nw_ref, wg_ref, wu_ref, wd_ref, fw_ref, o_ref, *, final):
    g = g_ref[...]
    m = (g[:, :D_MODEL] * _dot(ym_ref[...], pm_ref[...])
         + g[:, D_MODEL:] * _dot(yd_ref[...], pd_ref[...]))
    x2 = x_ref[...] + _dot(m.astype(BF16), wo_ref[...])
    y = _swiglu_half_step(x2, nw_ref, wg_ref, wu_ref, wd_ref)
    if final:
        y = _rms(y, fw_ref[...])
    o_ref[...] = y


def _merge_ffn(x1, ym, yd, g, pm, pd, wo, norm_w, wg, wu, wd, final_w, *, final, tm):
    n = x1.shape[0]

    def row(w):
        return pl.BlockSpec((tm, w), lambda i: (i, 0))

    weights = (pm, pd, wo, norm_w, wg, wu, wd, final_w)
    return pl.pallas_call(
        functools.partial(_merge_ffn_kernel, final=final),
        grid=(n // tm,),
        in_specs=[row(D_MODEL), row(ym.shape[1]), row(yd.shape[1]), row(g.shape[1])]
                 + [_resident(w.shape) for w in weights],
        out_specs=row(D_MODEL),
        out_shape=jax.ShapeDtypeStruct((n, D_MODEL), F32),
        compiler_params=pltpu.CompilerParams(dimension_semantics=("arbitrary",),
                                             vmem_limit_bytes=VMEM_LIMIT),
        name="merge_ffn",
    )(x1, ym, yd, g, *weights)


def _ffn_weights(wg, wu, wd):
    return wg.astype(BF16), wu.astype(BF16), wd.astype(BF16)


def _swap_halves(w):
    half = w.shape[-1] // 2
    return jnp.concatenate([w[..., half:], w[..., :half]], axis=-1)


def _inproj_weights(w_in, w_uq, w_ukv):
    d = w_in.shape[0]
    cuts = np.cumsum((MLA_Q_RANK, MLA_KV_RANK, MLA_ROPE, DIFF_WIDTH, DIFF_WIDTH, DIFF_WIDTH, D_MODEL))
    c_q, c_kv, kr, dq, dk, dv, gates = (w_in[:, a:b] for a, b in
                                        zip((0, *cuts[:-1]), cuts))
    gates = jnp.concatenate([gates, w_in[:, cuts[-1]:]], axis=1)

    def slot(w):
        return jnp.pad(w, ((0, 0), (MLA_NOPE, LANES - MLA_NOPE - MLA_ROPE)))

    dscale = DIFF_HEAD_DIM ** -0.5 * LOG2_E
    win = jnp.concatenate([c_q, c_kv, slot(kr), slot(_swap_halves(kr)), dq * dscale, dk, dv, gates],
                          axis=1).astype(BF16)

    qscale = (MLA_NOPE + MLA_ROPE) ** -0.5 * LOG2_E
    uq = w_uq.reshape(MLA_Q_RANK, MLA_HEADS, MLA_NOPE + MLA_ROPE) * qscale
    pad_r = LANES - MLA_NOPE - MLA_ROPE
    wuqa = jnp.pad(uq, ((0, 0), (0, 0), (0, pad_r)))
    wuqb = jnp.pad(_swap_halves(uq[..., MLA_NOPE:]), ((0, 0), (0, 0), (MLA_NOPE, pad_r)))
    ukv = w_ukv.reshape(MLA_KV_RANK, MLA_HEADS, MLA_NOPE + MLA_V)
    wuk = jnp.pad(ukv[..., :MLA_NOPE], ((0, 0), (0, 0), (0, LANES - MLA_NOPE)))
    wuv = jnp.pad(ukv[..., MLA_NOPE:], ((0, 0), (0, 0), (0, LANES - MLA_V)))
    flat = lambda w: w.reshape(w.shape[0], MLA_HEADS * LANES).astype(BF16)
    return win, flat(wuqa), flat(wuqb), flat(wuk), flat(wuv)


def _rotary_rows():
    def inv_freq(rot_dim):
        half = rot_dim // 2
        return np.power(np.float32(ROPE_THETA), -2.0 * np.arange(half, dtype=np.float32) / rot_dim)

    rows = np.zeros((N_ROT_ROWS, LANES), np.float32)
    hm = MLA_ROPE // 2
    m0 = MLA_NOPE
    rows[R_FREQ, m0:m0 + MLA_ROPE] = np.tile(inv_freq(MLA_ROPE), 2)
    rows[R_MLA, m0:m0 + MLA_ROPE] = 1.0
    rows[R_MLA_SIGN, m0:m0 + hm] = -1.0
    rows[R_MLA_SIGN, m0 + hm:m0 + MLA_ROPE] = 1.0
    hd = DIFF_ROT // 2
    rows[R_FREQ, :DIFF_ROT] = np.tile(inv_freq(DIFF_ROT), 2)
    rows[R_ROT_LO, :DIFF_ROT] = 1.0
    rows[R_ROT_HI, DIFF_HEAD_DIM:DIFF_HEAD_DIM + DIFF_ROT] = 1.0
    for base in (0, DIFF_HEAD_DIM):
        rows[R_UP, base:base + hd] = -1.0
        rows[R_DN, base + hd:base + DIFF_ROT] = 1.0
    rows[R_LO, :DIFF_HEAD_DIM] = 1.0
    rows[R_HI, DIFF_HEAD_DIM:] = 1.0
    return jnp.asarray(rows)


def kernel(x, positions, ffn1_norm, ffn1_w_gate, ffn1_w_up, ffn1_w_down, mix_norm, w_in, mla_q_norm, mla_w_uq, mla_kv_norm, mla_w_ukv, diff_lam_q1, diff_lam_k1, diff_lam_q2, diff_lam_k2, diff_subln, w_proj_mla, w_proj_diff, w_out, ffn2_norm, ffn2_w_gate, ffn2_w_up, ffn2_w_down, final_norm):
    depth = ffn1_norm.shape[0]
    b, s, d = x.shape
    n = b * s
    x2d = x.reshape(n, d)
    posb = jnp.broadcast_to(positions.astype(F32).reshape(n, 1), (n, LANES))
    rot = _rotary_rows()
    final_w = final_norm.reshape(1, d)

    for l in range(depth):
        lam_init = 0.8 - 0.6 * math.exp(-0.3 * l)
        f1 = _ffn_weights(ffn1_w_gate[l], ffn1_w_up[l], ffn1_w_down[l])
        f2 = _ffn_weights(ffn2_w_gate[l], ffn2_w_up[l], ffn2_w_down[l])
        win, wuqa, wuqb, wuk, wuv = _inproj_weights(w_in[l], mla_w_uq[l], mla_w_ukv[l])
        pm = w_proj_mla[l].astype(BF16)
        pd = w_proj_diff[l].astype(BF16)
        wo = w_out[l].astype(BF16)
        lam_rows = jnp.stack([diff_lam_q1[l], diff_lam_k1[l], diff_lam_q2[l], diff_lam_k2[l]])

        x1 = _ffn(x2d, ffn1_norm[l].reshape(1, d), *f1, tm=512)
        qm, km, vm, qd1, qd2, kd, vd, g = _inproj(
            x1, posb, mix_norm[l].reshape(1, d), win, mla_q_norm[l].reshape(1, -1), wuqa, wuqb,
            mla_kv_norm[l].reshape(1, -1), wuk, wuv, rot, tm=512)

        def r3(t):
            return t.reshape(b, s, t.shape[1])

        ym = _flash([r3(qm)], r3(km), r3(vm), [], groups=MLA_HEADS // 2, heads=2, splits=2, rows=512,
                    l_row=MLA_V, acc_rows=MLA_V + BF16_ROWS, finish=functools.partial(_mla_finish, splits=2),
                    name="mla_attn")
        yd = _flash([r3(qd1), r3(qd2)], r3(kd), r3(vd), [lam_rows, diff_subln[l].reshape(1, -1)],
                    groups=DIFF_HEADS, heads=1, splits=2, rows=512, l_row=LANES, acc_rows=LANES + BF16_ROWS,
                    finish=functools.partial(_diff_finish, lam_init=lam_init), name="diff_attn")
        x2d = _merge_ffn(x1, ym.reshape(n, -1), yd.reshape(n, -1), g, pm, pd, wo,
                         ffn2_norm[l].reshape(1, d), *f2, final_w, final=(l == depth - 1), tm=512)
    return x2d.reshape(b, s, d)
```

```python
import functools
import math

import numpy as np
import jax
import jax.numpy as jnp
from jax import lax
from jax.experimental import pallas as pl
from jax.experimental.pallas import tpu as pltpu

F32 = jnp.float32
BF16 = jnp.bfloat16

D_MODEL = 1024
D_FF = 2816
ROPE_THETA = 500000.0
EPS = 1e-6
LOG2_E = math.log2(math.e)

MLA_HEADS = 8
MLA_Q_RANK = 256
MLA_KV_RANK = 128
MLA_NOPE = 64
MLA_ROPE = 32
MLA_V = 64

DIFF_HEADS = 4
DIFF_HEAD_DIM = 64
DIFF_ROT = DIFF_HEAD_DIM // 4
DIFF_WIDTH = DIFF_HEADS * 2 * DIFF_HEAD_DIM

LANES = 128
BF16_ROWS = 16
FF_CHUNK = 256
N_FF_CHUNKS = D_FF // FF_CHUNK
VMEM_LIMIT = 56 * 1024 * 1024

C_CQ = 0
C_CKV = C_CQ + MLA_Q_RANK
C_KRA = C_CKV + MLA_KV_RANK
C_KRB = C_KRA + LANES
C_DQ = C_KRB + LANES
C_DK = C_DQ + DIFF_WIDTH
C_DV = C_DK + DIFF_WIDTH
C_GATE = C_DV + DIFF_WIDTH
N_IN_COLS = C_GATE + 2 * D_MODEL

(R_FREQ, R_MLA, R_MLA_SIGN, R_ROT_LO, R_ROT_HI, R_UP, R_DN, R_LO, R_HI) = range(9)
N_ROT_ROWS = 16


def _rms(xf, w):
    ms = jnp.mean(xf * xf, axis=-1, keepdims=True)
    return xf * lax.rsqrt(ms + EPS) * w


def _dot(a, b):
    return jnp.dot(a, b, preferred_element_type=F32)


def _resident(shape):
    nd = len(shape)
    return pl.BlockSpec(shape, lambda *_: (0,) * nd, pipeline_mode=pl.Buffered(1))


def _swiglu_half_step(x, nw_ref, wg_ref, wu_ref, wd_ref):
    xn = _rms(x, nw_ref[...]).astype(BF16)
    acc = None
    for c in range(N_FF_CHUNKS):
        ff = slice(c * FF_CHUNK, (c + 1) * FF_CHUNK)
        g = _dot(xn, wg_ref[:, ff])
        u = _dot(xn, wu_ref[:, ff])
        a = (g * jax.nn.sigmoid(g) * u).astype(BF16)
        d = _dot(a, wd_ref[ff, :])
        acc = d if acc is None else acc + d
    return x + 0.5 * acc


def _ffn_inproj_kernel(x_ref, pos_ref, nw_ref, wg_ref, wu_ref, wd_ref,
                       mixw_ref, win_ref, qnw_ref, wuqa_ref, wuqb_ref, kvnw_ref, wuk_ref, wuv_ref, rot_ref,
                       x1_ref, qm_ref, km_ref, vm_ref, qd1_ref, qd2_ref, kd_ref, vd_ref, g_ref):
    x1 = _swiglu_half_step(x_ref[...], nw_ref, wg_ref, wu_ref, wd_ref)
    x1_ref[...] = x1
    h = _rms(x1, mixw_ref[...]).astype(BF16)
    rot = rot_ref[...]
    ang = pos_ref[...] * rot[R_FREQ:R_FREQ + 1]
    cos_a, sin_a = jnp.cos(ang), jnp.sin(ang)
    cos_m = jnp.where(rot[R_MLA:R_MLA + 1] > 0, cos_a, 1.0)
    sin_m = sin_a * rot[R_MLA_SIGN:R_MLA_SIGN + 1]
    cos_hi = pltpu.roll(cos_a, DIFF_HEAD_DIM, axis=1)
    sin_hi = pltpu.roll(sin_a, DIFF_HEAD_DIM, axis=1)
    lo, hi = rot[R_LO:R_LO + 1], rot[R_HI:R_HI + 1]
    rot_lo, rot_hi = rot[R_ROT_LO:R_ROT_LO + 1], rot[R_ROT_HI:R_ROT_HI + 1]
    cos_d = jnp.where(rot_lo > 0, cos_a, jnp.where(rot_hi > 0, cos_hi, 1.0))
    sin_d = sin_a * rot_lo + sin_hi * rot_hi
    sin_up = sin_d * rot[R_UP:R_UP + 1]
    sin_dn = sin_d * rot[R_DN:R_DN + 1]

    z0 = _dot(h, win_ref[:, C_CQ:C_DQ])
    cqn = _rms(z0[:, C_CQ:C_CKV], qnw_ref[...]).astype(BF16)
    kvn = _rms(z0[:, C_CKV:C_KRA], kvnw_ref[...]).astype(BF16)
    k_rope = z0[:, C_KRA:C_KRB] * cos_m + z0[:, C_KRB:C_DQ] * sin_m

    qa = _dot(cqn, wuqa_ref[...])
    qb = _dot(cqn, wuqb_ref[...])
    ka = _dot(kvn, wuk_ref[...])
    for hd in range(MLA_HEADS):
        sl = slice(hd * LANES, (hd + 1) * LANES)
        qm_ref[:, sl] = (qa[:, sl] * cos_m + qb[:, sl] * sin_m).astype(BF16)
        km_ref[:, sl] = (ka[:, sl] + k_rope).astype(BF16)
    vm = _dot(kvn, wuv_ref[...])
    lane = lax.broadcasted_iota(jnp.int32, vm.shape, 1) % LANES
    vm_ref[...] = jnp.where(lane == MLA_V, 1.0, vm).astype(BF16)

    half = DIFF_ROT // 2

    head_lanes = [slice(hd * LANES, (hd + 1) * LANES) for hd in range(DIFF_HEADS)]

    def diff_rope(x):
        up = pltpu.roll(x, DIFF_WIDTH - half, axis=1)
        dn = pltpu.roll(x, half, axis=1)
        return [x[:, sl] * cos_d + up[:, sl] * sin_up + dn[:, sl] * sin_dn for sl in head_lanes]

    dq = diff_rope(_dot(h, win_ref[:, C_DQ:C_DK]))
    dk = diff_rope(_dot(h, win_ref[:, C_DK:C_DV]))
    for sl, qr, kr in zip(head_lanes, dq, dk):
        qd1_ref[:, sl] = (qr * lo).astype(BF16)
        qd2_ref[:, sl] = (qr * hi).astype(BF16)
        kd_ref[:, sl] = kr.astype(BF16)
    vd_ref[...] = _dot(h, win_ref[:, C_DV:C_GATE]).astype(BF16)
    g_ref[...] = jax.nn.sigmoid(_dot(h, win_ref[:, C_GATE:N_IN_COLS])).astype(BF16)


def _ffn_inproj(x, posb, ffn_w, proj_w, *, tm):
    n = x.shape[0]

    def row(w):
        return pl.BlockSpec((tm, w), lambda i: (i, 0))

    wide = MLA_HEADS * LANES
    out_widths = (wide, wide, wide, DIFF_WIDTH, DIFF_WIDTH, DIFF_WIDTH, DIFF_WIDTH, 2 * D_MODEL)
    weights = (*ffn_w, *proj_w)
    return pl.pallas_call(
        _ffn_inproj_kernel,
        grid=(n // tm,),
        in_specs=[row(D_MODEL), row(LANES)] + [_resident(w.shape) for w in weights],
        out_specs=[row(D_MODEL)] + [row(w) for w in out_widths],
        out_shape=[jax.ShapeDtypeStruct((n, D_MODEL), F32)]
                  + [jax.ShapeDtypeStruct((n, w), BF16) for w in out_widths],
        compiler_params=pltpu.CompilerParams(dimension_semantics=("arbitrary",),
                                             vmem_limit_bytes=VMEM_LIMIT),
        name="ffn_inproj",
    )(x, posb, *weights)


class _Chain:
    def __init__(self, q, head, off, refs):
        self.q, self.head, self.off = q, head, off
        self.m, self.acc = refs[0:2]
        self.s = refs[2:4]
        self.smax = refs[4:6]


_CHAIN_REFS = 6


def _score_tile(ch, slot, k, masked):
    s = lax.dot_general(k, ch.q, (((1,), (1,)), ((), ())), preferred_element_type=F32)
    if masked:
        kidx = lax.broadcasted_iota(jnp.int32, s.shape, 0)
        qidx = lax.broadcasted_iota(jnp.int32, s.shape, 1)
        s = jnp.where(kidx <= qidx, s, -jnp.inf)
    ch.s[slot][...] = s
    ch.smax[slot][...] = jnp.max(s, axis=0, keepdims=True)


def _softmax_update(ch, slot, vt):
    m_prev = ch.m[...]
    m_new = jnp.maximum(m_prev, ch.smax[slot][...])
    alpha = jnp.exp2(m_prev - m_new)
    p = jnp.exp2(ch.s[slot][...] - m_new)
    ch.acc[...] = alpha * ch.acc[...] + _dot(vt, p.astype(BF16))
    ch.m[...] = m_new


def _flash_kernel(*refs, n_q, heads, splits, rows, l_row, acc_rows, finish):
    assert splits % 2 == 0
    n_chains = n_q * heads * splits
    q_refs = refs[:n_q]
    k_ref, v_ref = refs[n_q], refs[n_q + 1]
    n_scratch = _CHAIN_REFS * n_chains
    extra = refs[n_q + 2:len(refs) - 1 - n_scratch]
    o_ref = refs[len(refs) - 1 - n_scratch]
    scratch = iter(refs[len(refs) - n_scratch:])
    chains = [_Chain(q_refs[i][0, c * rows:(c + 1) * rows, h * LANES:(h + 1) * LANES], h, c,
                     [next(scratch) for _ in range(_CHAIN_REFS)])
              for i in range(n_q) for h in range(heads) for c in range(splits)]
    for ch in chains:
        ch.m[...] = jnp.full_like(ch.m, -jnp.inf)
        ch.acc[...] = jnp.zeros_like(ch.acc)

    base = pl.program_id(2) * splits

    def tile(ref, j, head):
        return ref[0, pl.ds(pl.multiple_of(j * rows, rows), rows), head * LANES:(head + 1) * LANES]

    def value_rows(j, head):
        vt = tile(v_ref, j, head).T
        if l_row < LANES:
            return vt[:acc_rows]
        return jnp.concatenate([vt, jnp.ones((acc_rows - LANES, rows), BF16)], axis=0)

    def scores(ch, j, slot, diag=None):
        if diag is None or ch.off >= diag:
            _score_tile(ch, slot, tile(k_ref, j, ch.head), masked=(ch.off == diag))

    def update(ch, j, slot, diag=None):
        if diag is None or ch.off >= diag:
            _softmax_update(ch, slot, value_rows(j, ch.head))

    def step(j, slot, diag=None, next_diag=None, last=False):
        for ch in chains:
            if not last:
                scores(ch, j + 1, 1 - slot, next_diag)
            update(ch, j, slot, diag)

    def two_tiles(jj, carry):
        step(2 * jj, 0)
        step(2 * jj + 1, 1)
        return carry

    @pl.when(base > 0)
    def _():
        for ch in chains:
            scores(ch, 0, 0)
        lax.fori_loop(0, base // 2 - 1, two_tiles, 0)
        step(base - 2, 0)
        step(base - 1, 1, next_diag=0)

    @pl.when(base == 0)
    def _():
        for ch in chains:
            scores(ch, 0, 0, diag=0)

    for d in range(splits):
        step(base + d, d % 2, diag=d, next_diag=d + 1, last=(d + 1 == splits))

    outs = []
    for ch in chains:
        acc = ch.acc[...]
        outs.append(acc[:min(LANES, acc_rows)] / acc[l_row:l_row + 1])
    finish(outs, o_ref, *extra)


def _flash(qs, k, v, extra, *, groups, heads, splits, rows, l_row, acc_rows, finish, name):
    b, s, _ = k.shape
    n_q = len(qs)
    tq = splits * rows
    q_spec = pl.BlockSpec((1, tq, heads * LANES), lambda bi, gi, qi: (bi, qi, gi))
    kv_spec = pl.BlockSpec((1, s, heads * LANES), lambda bi, gi, qi: (bi, 0, gi))
    chain_scratch = ([pltpu.VMEM((1, rows), F32), pltpu.VMEM((acc_rows, rows), F32)]
                     + [pltpu.VMEM((rows, rows), F32)] * 2 + [pltpu.VMEM((1, rows), F32)] * 2)
    return pl.pallas_call(
        functools.partial(_flash_kernel, n_q=n_q, heads=heads, splits=splits, rows=rows, l_row=l_row,
                          acc_rows=acc_rows, finish=finish),
        grid=(b, groups, s // tq),
        in_specs=[q_spec] * n_q + [kv_spec, kv_spec] + [_resident(e.shape) for e in extra],
        out_specs=pl.BlockSpec((1, tq, LANES), lambda bi, gi, qi: (bi, qi, gi)),
        out_shape=jax.ShapeDtypeStruct((b, s, groups * LANES), BF16),
        scratch_shapes=chain_scratch * (n_q * heads * splits),
        compiler_params=pltpu.CompilerParams(
            dimension_semantics=("arbitrary", "arbitrary", "arbitrary"),
            vmem_limit_bytes=VMEM_LIMIT),
        name=name,
    )(*qs, k, v, *extra)


def _mla_finish(outs, o_ref, *, splits):
    rows = outs[0].shape[1]
    for c in range(splits):
        pair = jnp.concatenate([outs[c][:MLA_V], outs[splits + c][:MLA_V]], axis=0)
        o_ref[0, c * rows:(c + 1) * rows, :] = pair.T.astype(o_ref.dtype)


def _diff_finish(outs, o_ref, lam_ref, subln_ref, *, lam_init):
    lam_v = lam_ref[...]
    e1 = jnp.exp(jnp.sum(lam_v[0:1] * lam_v[1:2], axis=1, keepdims=True))
    e2 = jnp.exp(jnp.sum(lam_v[2:3] * lam_v[3:4], axis=1, keepdims=True))
    lam = e1 - e2 + lam_init
    splits = len(outs) // 2
    rows = outs[0].shape[1]
    for c in range(splits):
        o = (outs[c] - lam * outs[splits + c]).T
        o_ref[0, c * rows:(c + 1) * rows, :] = (_rms(o, subln_ref[...]) * (1.0 - lam_init)).astype(o_ref.dtype)


def _merge_ffn_kernel(x_ref, ym_ref, yd_ref, g_ref, pm_ref, pd_ref, wo_ref,
                      nw_ref, wg_ref, wu_ref, wd_ref, fw_ref, o_ref, *, final):
    g = g_ref[...]
    m = (g[:, :D_MODEL] * _dot(ym_ref[...], pm_ref[...])
         + g[:, D_MODEL:] * _dot(yd_ref[...], pd_ref[...]))
    x2 = x_ref[...] + _dot(m.astype(BF16), wo_ref[...])
    y = _swiglu_half_step(x2, nw_ref, wg_ref, wu_ref, wd_ref)
    if final:
        y = _rms(y, fw_ref[...])
    o_ref[...] = y


def _merge_ffn(x1, ym, yd, g, pm, pd, wo, norm_w, wg, wu, wd, final_w, *, final, tm):
    n = x1.shape[0]

    def row(w):
        return pl.BlockSpec((tm, w), lambda i: (i, 0))

    weights = (pm, pd, wo, norm_w, wg, wu, wd, final_w)
    return pl.pallas_call(
        functools.partial(_merge_ffn_kernel, final=final),
        grid=(n // tm,),
        in_specs=[row(D_MODEL), row(ym.shape[1]), row(yd.shape[1]), row(g.shape[1])]
                 + [_resident(w.shape) for w in weights],
        out_specs=row(D_MODEL),
        out_shape=jax.ShapeDtypeStruct((n, D_MODEL), F32),
        compiler_params=pltpu.CompilerParams(dimension_semantics=("arbitrary",),
                                             vmem_limit_bytes=VMEM_LIMIT),
        name="merge_ffn",
    )(x1, ym, yd, g, *weights)


def _ffn_weights(wg, wu, wd):
    return wg.astype(BF16), wu.astype(BF16), wd.astype(BF16)


def _swap_halves(w):
    half = w.shape[-1] // 2
    return jnp.concatenate([w[..., half:], w[..., :half]], axis=-1)


def _inproj_weights(w_in, w_uq, w_ukv):
    d = w_in.shape[0]
    cuts = np.cumsum((MLA_Q_RANK, MLA_KV_RANK, MLA_ROPE, DIFF_WIDTH, DIFF_WIDTH, DIFF_WIDTH, D_MODEL))
    c_q, c_kv, kr, dq, dk, dv, gates = (w_in[:, a:b] for a, b in
                                        zip((0, *cuts[:-1]), cuts))
    gates = jnp.concatenate([gates, w_in[:, cuts[-1]:]], axis=1)

    def slot(w):
        return jnp.pad(w, ((0, 0), (MLA_NOPE, LANES - MLA_NOPE - MLA_ROPE)))

    dscale = DIFF_HEAD_DIM ** -0.5 * LOG2_E
    win = jnp.concatenate([c_q, c_kv, slot(kr), slot(_swap_halves(kr)), dq * dscale, dk, dv, gates],
                          axis=1).astype(BF16)

    qscale = (MLA_NOPE + MLA_ROPE) ** -0.5 * LOG2_E
    uq = w_uq.reshape(MLA_Q_RANK, MLA_HEADS, MLA_NOPE + MLA_ROPE) * qscale
    pad_r = LANES - MLA_NOPE - MLA_ROPE
    wuqa = jnp.pad(uq, ((0, 0), (0, 0), (0, pad_r)))
    wuqb = jnp.pad(_swap_halves(uq[..., MLA_NOPE:]), ((0, 0), (0, 0), (MLA_NOPE, pad_r)))
    ukv = w_ukv.reshape(MLA_KV_RANK, MLA_HEADS, MLA_NOPE + MLA_V)
    wuk = jnp.pad(ukv[..., :MLA_NOPE], ((0, 0), (0, 0), (0, LANES - MLA_NOPE)))
    wuv = jnp.pad(ukv[..., MLA_NOPE:], ((0, 0), (0, 0), (0, LANES - MLA_V)))
    flat = lambda w: w.reshape(w.shape[0], MLA_HEADS * LANES).astype(BF16)
    return win, flat(wuqa), flat(wuqb), flat(wuk), flat(wuv)


def _rotary_rows():
    def inv_freq(rot_dim):
        half = rot_dim // 2
        return np.power(np.float32(ROPE_THETA), -2.0 * np.arange(half, dtype=np.float32) / rot_dim)

    rows = np.zeros((N_ROT_ROWS, LANES), np.float32)
    hm = MLA_ROPE // 2
    m0 = MLA_NOPE
    rows[R_FREQ, m0:m0 + MLA_ROPE] = np.tile(inv_freq(MLA_ROPE), 2)
    rows[R_MLA, m0:m0 + MLA_ROPE] = 1.0
    rows[R_MLA_SIGN, m0:m0 + hm] = -1.0
    rows[R_MLA_SIGN, m0 + hm:m0 + MLA_ROPE] = 1.0
    hd = DIFF_ROT // 2
    rows[R_FREQ, :DIFF_ROT] = np.tile(inv_freq(DIFF_ROT), 2)
    rows[R_ROT_LO, :DIFF_ROT] = 1.0
    rows[R_ROT_HI, DIFF_HEAD_DIM:DIFF_HEAD_DIM + DIFF_ROT] = 1.0
    for base in (0, DIFF_HEAD_DIM):
        rows[R_UP, base:base + hd] = -1.0
        rows[R_DN, base + hd:base + DIFF_ROT] = 1.0
    rows[R_LO, :DIFF_HEAD_DIM] = 1.0
    rows[R_HI, DIFF_HEAD_DIM:] = 1.0
    return jnp.asarray(rows)


def kernel(x, positions, ffn1_norm, ffn1_w_gate, ffn1_w_up, ffn1_w_down, mix_norm, w_in, mla_q_norm, mla_w_uq, mla_kv_norm, mla_w_ukv, diff_lam_q1, diff_lam_k1, diff_lam_q2, diff_lam_k2, diff_subln, w_proj_mla, w_proj_diff, w_out, ffn2_norm, ffn2_w_gate, ffn2_w_up, ffn2_w_down, final_norm):
    depth = ffn1_norm.shape[0]
    b, s, d = x.shape
    n = b * s
    x2d = x.reshape(n, d)
    posb = jnp.broadcast_to(positions.astype(F32).reshape(n, 1), (n, LANES))
    rot = _rotary_rows()
    final_w = final_norm.reshape(1, d)

    for l in range(depth):
        lam_init = 0.8 - 0.6 * math.exp(-0.3 * l)
        f1 = _ffn_weights(ffn1_w_gate[l], ffn1_w_up[l], ffn1_w_down[l])
        f2 = _ffn_weights(ffn2_w_gate[l], ffn2_w_up[l], ffn2_w_down[l])
        win, wuqa, wuqb, wuk, wuv = _inproj_weights(w_in[l], mla_w_uq[l], mla_w_ukv[l])
        pm = w_proj_mla[l].astype(BF16)
        pd = w_proj_diff[l].astype(BF16)
        wo = w_out[l].astype(BF16)
        lam_rows = jnp.stack([diff_lam_q1[l], diff_lam_k1[l], diff_lam_q2[l], diff_lam_k2[l]])

        x1, qm, km, vm, qd1, qd2, kd, vd, g = _ffn_inproj(
            x2d, posb, (ffn1_norm[l].reshape(1, d), *f1),
            (mix_norm[l].reshape(1, d), win, mla_q_norm[l].reshape(1, -1), wuqa, wuqb,
             mla_kv_norm[l].reshape(1, -1), wuk, wuv, rot), tm=512)

        def r3(t):
            return t.reshape(b, s, t.shape[1])

        ym = _flash([r3(qm)], r3(km), r3(vm), [], groups=MLA_HEADS // 2, heads=2, splits=2, rows=512,
                    l_row=MLA_V, acc_rows=MLA_V + BF16_ROWS, finish=functools.partial(_mla_finish, splits=2),
                    name="mla_attn")
        yd = _flash([r3(qd1), r3(qd2)], r3(kd), r3(vd), [lam_rows, diff_subln[l].reshape(1, -1)],
                    groups=DIFF_HEADS, heads=1, splits=2, rows=512, l_row=LANES, acc_rows=LANES + BF16_ROWS,
                    finish=functools.partial(_diff_finish, lam_init=lam_init), name="diff_attn")
        x2d = _merge_ffn(x1, ym.reshape(n, -1), yd.reshape(n, -1), g, pm, pd, wo,
                         ffn2_norm[l].reshape(1, d), *f2, final_w, final=(l == depth - 1), tm=512)
    return x2d.reshape(b, s, d)
```
